```python
import math
import jax, jax.numpy as jnp
from jax import lax
import numpy as np

D_MODEL = 4096
BATCH = 2
SEQ = 4096
DEPTH = 4

D_FF = 6144
COND_HIDDEN = 256
N_MOD = 9
RET_WIDTH = D_MODEL // 2
RET_HEADS = 8
RET_HEAD_DIM = RET_WIDTH // RET_HEADS
RET_CHUNK = 128
ROPE_BASE = 10000.0
LRU_WIDTH = D_MODEL // 2
LRU_BLOCKS = 16
LRU_BLOCK = LRU_WIDTH // LRU_BLOCKS
LRU_CONV = 4
LRU_C = 8.0
EVEN_IN = 4 * RET_WIDTH + 2 * LRU_WIDTH
EVEN_MIX = RET_WIDTH + LRU_WIDTH
HY_WIDTH = D_MODEL
HY_ORDER = 2
HY_SHORT = 3
HY_EMB = 33
HY_FILTER_HIDDEN = 64
HY_FAST_DECAY = 0.3
HY_SLOW_DECAY = 1.5
HY_TARGET = 1e-2
EPS = 1e-6
N_EVEN = (DEPTH + 1) // 2
N_ODD = DEPTH // 2

kernel_name = "hybrid_retention_rglru_hyena_macaron_adaln"


def rmsnorm(x, g):
    xf = x.astype(jnp.float32)
    y = xf * lax.rsqrt(jnp.mean(xf * xf, axis=-1, keepdims=True) + EPS)
    return (y * g.astype(jnp.float32)).astype(x.dtype)


def swiglu(h, w1, w3, w2):
    return (jax.nn.silu(h @ w1) * (h @ w3)) @ w2


def depthwise_conv(x, w, b, pad):
    C = w.shape[1]
    y = lax.conv_general_dilated(x, w[:, None, :].astype(x.dtype), window_strides=(1,), padding=[pad],
                                 dimension_numbers=('NWC', 'WIO', 'NWC'), feature_group_count=C)
    return y + b.astype(x.dtype)


def rotary(x, pos):
    half = x.shape[-1] // 2
    inv = ROPE_BASE ** (-jnp.arange(half, dtype=jnp.float32) / half)
    ang = pos.astype(jnp.float32)[:, None] * inv[None, :]
    cos = jnp.cos(ang)[None, :, None, :]
    sin = jnp.sin(ang)[None, :, None, :]
    x1, x2 = x[..., :half], x[..., half:]
    return jnp.concatenate([x1 * cos - x2 * sin, x1 * sin + x2 * cos], axis=-1)


def retention(q, k, v):
    B, L, H, Dh = q.shape
    cs = RET_CHUNK
    nc = L // cs
    log_g = jnp.log1p(-(2.0 ** (-5.0 - jnp.arange(H, dtype=jnp.float32))))
    idx = jnp.arange(cs, dtype=jnp.float32)
    inner_mask = jnp.exp(jnp.abs(idx[:, None] - idx[None, :])[None] * log_g[:, None, None])
    zeta_f = jnp.exp((cs - 1.0 - idx)[:, None] * log_g[None])
    zeta_b = jnp.exp(idx[:, None] * log_g[None])
    xi_f = jnp.exp((idx + 1.0)[:, None] * log_g[None])
    xi_b = jnp.exp((cs - idx)[:, None] * log_g[None])
    chunk_decay = jnp.exp(cs * log_g)[None, :, None, None]
    qc = q.reshape(B, nc, cs, H, Dh)
    kc = k.reshape(B, nc, cs, H, Dh)
    vc = v.reshape(B, nc, cs, H, Dh)
    scores = jnp.einsum('bcnhd,bcmhd->bchnm', qc, kc) * inner_mask
    y = jnp.einsum('bchnm,bcmhe->bcnhe', scores, vc)
    kv_f = jnp.einsum('bcmhd,bcmhe,mh->cbhde', kc, vc, zeta_f)
    kv_b = jnp.einsum('bcmhd,bcmhe,mh->cbhde', kc, vc, zeta_b)

    def step(state, kv):
        return chunk_decay * state + kv, state

    init = jnp.zeros((B, H, Dh, Dh), jnp.float32)
    _, past = lax.scan(step, init, kv_f)
    _, future = lax.scan(step, init, kv_b, reverse=True)
    y = (y + jnp.einsum('bcnhd,cbhde->bcnhe', qc, past) * xi_f[:, :, None]
         + jnp.einsum('bcnhd,cbhde->bcnhe', qc, future) * xi_b[:, :, None])
    return y.reshape(B, L, H, Dh)


def block_diag(x, w, b):
    B, L, W = x.shape
    xb = x.reshape(B, L, LRU_BLOCKS, LRU_BLOCK)
    return jnp.einsum('blni,nio->blno', xb, w.astype(jnp.float32)).reshape(B, L, W) + b.astype(jnp.float32)


def rglru(x, a_w, a_b, i_w, i_b, lam, reverse):
    r = jax.nn.sigmoid(block_diag(x, a_w, a_b))
    i = jax.nn.sigmoid(block_diag(x, i_w, i_b))
    log_a = -LRU_C * r * jax.nn.softplus(-lam.astype(jnp.float32))
    a = jnp.exp(log_a)
    b = jnp.sqrt(-jnp.expm1(2.0 * log_a)) * (i * x)

    def combine(e1, e2):
        a1, b1 = e1
        a2, b2 = e2
        return a1 * a2, a2 * b1 + b2

    _, h = lax.associative_scan(combine, (a, b), reverse=reverse, axis=1)
    return h


def even_mixer(h, w_in, w_out, gn_g, conv_w, conv_b, a_w, a_b, i_w, i_b, lam):
    B, L, _ = h.shape
    dt = h.dtype
    u = h @ w_in
    q, k, v, g, xr, xg = jnp.split(u, [RET_WIDTH, 2 * RET_WIDTH, 3 * RET_WIDTH, 4 * RET_WIDTH,
                                      4 * RET_WIDTH + LRU_WIDTH], axis=-1)
    heads = lambda t: t.astype(jnp.float32).reshape(B, L, RET_HEADS, RET_HEAD_DIM)
    pos = jnp.arange(L)
    qh = rotary(heads(q), pos)
    kh = rotary(heads(k), pos) * (RET_HEAD_DIM ** -0.5)
    y = retention(qh, kh, heads(v))
    mu = jnp.mean(y, axis=-1, keepdims=True)
    var = jnp.mean(jnp.square(y - mu), axis=-1, keepdims=True)
    y = ((y - mu) * lax.rsqrt(var + EPS)).reshape(B, L, RET_WIDTH) * gn_g.astype(jnp.float32)
    ret_out = jax.nn.silu(g.astype(jnp.float32)) * y
    xc = depthwise_conv(xr, conv_w, conv_b, (2, 1)).astype(jnp.float32)
    lru = (rglru(xc, a_w[0], a_b[0], i_w[0], i_b[0], lam[0], False)
           + rglru(xc, a_w[1], a_b[1], i_w[1], i_b[1], lam[1], True))
    lru_out = lru * jax.nn.gelu(xg.astype(jnp.float32))
    mix = jnp.concatenate([ret_out, lru_out], axis=-1).astype(dt)
    return mix @ w_out


def hyena_filters(L, w1, b1, w2, b2, w3, freq):
    f32 = jnp.float32
    t = jnp.linspace(0.0, 1.0, L, dtype=f32)[:, None]
    bands = (HY_EMB - 1) // 2
    f = jnp.linspace(1e-4, bands - 1, bands, dtype=f32)[None, :]
    w = 2.0 * math.pi * jnp.arange(L, dtype=f32)[:, None] / L
    z = jnp.concatenate([t, jnp.cos(f * w), -jnp.sin(f * w)], axis=-1)
    freq = freq.astype(f32)
    hid = jnp.sin(freq[0] * (z @ w1.astype(f32) + b1.astype(f32)))
    hid = jnp.sin(freq[1] * (hid @ w2.astype(f32) + b2.astype(f32)))
    filt = hid @ w3.astype(f32)
    max_decay = math.log(HY_TARGET) / HY_FAST_DECAY
    min_decay = math.log(HY_TARGET) / HY_SLOW_DECAY
    deltas = jnp.linspace(min_decay, max_decay, HY_WIDTH, dtype=f32)
    window = jnp.exp(-t * jnp.abs(deltas)[None, :])
    return filt.reshape(L, HY_ORDER, 2, HY_WIDTH) * window[:, None, None, :]


def two_sided_spectrum(h_fwd, h_bwd):
    L, C = h_fwd.shape
    k = jnp.concatenate([h_fwd, jnp.zeros((1, C), h_fwd.dtype), h_bwd[:0:-1]], axis=0)
    return jnp.fft.rfft(k, axis=0)


def hyena_mixer(h, w_in, b_in, conv_w, conv_b, f_w1, f_b1, f_w2, f_b2, f_w3, f_freq, bias, w_out):
    B, L, _ = h.shape
    dt = h.dtype
    u = depthwise_conv(h @ w_in + b_in, conv_w, conv_b, (1, 1)).astype(jnp.float32)
    v, x1, x2 = jnp.split(u, 3, axis=-1)
    filt = hyena_filters(L, f_w1, f_b1, f_w2, f_b2, f_w3, f_freq)
    bias = bias.astype(jnp.float32)
    z = v
    for o, gate in enumerate((x1, x2)):
        spec = two_sided_spectrum(filt[:, o, 0], filt[:, o, 1])
        zf = jnp.fft.rfft(z, n=2 * L, axis=1)
        conv = jnp.fft.irfft(zf * spec[None], n=2 * L, axis=1)[:, :L]
        z = gate * (conv + bias[o] * z)
    return z.astype(dt) @ w_out


def setup_inputs(seed: int = 0) -> dict:
    key = jax.random.key(seed)
    ks = iter(jax.random.split(key, 40))
    f32 = jnp.float32

    def nrm(shape, scale):
        return jax.random.normal(next(ks), shape, f32) * scale

    def gain(shape):
        return 1.0 + nrm(shape, 0.02)

    u = jax.random.uniform(next(ks), (N_EVEN, 2, LRU_WIDTH), f32, minval=0.9, maxval=0.999)
    s = u ** (1.0 / LRU_C)
    lru_lambda = jnp.log(s) - jnp.log1p(-s)
    return {
        "x": nrm((BATCH, SEQ, D_MODEL), 1.0),
        "c": nrm((BATCH, D_MODEL), 1.0),
        "cond_w": nrm((D_MODEL, COND_HIDDEN), D_MODEL ** -0.5),
        "cond_b": nrm((COND_HIDDEN,), 0.01),
        "mod_w": nrm((DEPTH, COND_HIDDEN, N_MOD * D_MODEL), 0.1 * COND_HIDDEN ** -0.5),
        "mod_b": nrm((DEPTH, N_MOD * D_MODEL), 0.01),
        "norm_g": gain((DEPTH, 3, D_MODEL)),
        "ffn_w1": nrm((DEPTH, 2, D_MODEL, D_FF), D_MODEL ** -0.5),
        "ffn_w3": nrm((DEPTH, 2, D_MODEL, D_FF), D_MODEL ** -0.5),
        "ffn_w2": nrm((DEPTH, 2, D_FF, D_MODEL), D_FF ** -0.5),
        "ev_in_w": nrm((N_EVEN, D_MODEL, EVEN_IN), D_MODEL ** -0.5),
        "ev_out_w": nrm((N_EVEN, EVEN_MIX, D_MODEL), EVEN_MIX ** -0.5),
        "ret_gn_g": gain((N_EVEN, RET_WIDTH)),
        "lru_conv_w": nrm((N_EVEN, LRU_CONV, LRU_WIDTH), LRU_CONV ** -0.5),
        "lru_conv_b": nrm((N_EVEN, LRU_WIDTH), 0.01),
        "lru_a_w": nrm((N_EVEN, 2, LRU_BLOCKS, LRU_BLOCK, LRU_BLOCK), LRU_BLOCK ** -0.5),
        "lru_a_b": nrm((N_EVEN, 2, LRU_WIDTH), 0.01),
        "lru_i_w": nrm((N_EVEN, 2, LRU_BLOCKS, LRU_BLOCK, LRU_BLOCK), LRU_BLOCK ** -0.5),
        "lru_i_b": nrm((N_EVEN, 2, LRU_WIDTH), 0.01),
        "lru_lambda": lru_lambda,
        "hy_in_w": nrm((N_ODD, D_MODEL, 3 * HY_WIDTH), D_MODEL ** -0.5),
        "hy_in_b": nrm((N_ODD, 3 * HY_WIDTH), 0.01),
        "hy_conv_w": nrm((N_ODD, HY_SHORT, 3 * HY_WIDTH), HY_SHORT ** -0.5),
        "hy_conv_b": nrm((N_ODD, 3 * HY_WIDTH), 0.01),
        "hy_f_w1": nrm((N_ODD, HY_EMB, HY_FILTER_HIDDEN), HY_EMB ** -0.5),
        "hy_f_b1": nrm((N_ODD, HY_FILTER_HIDDEN), 0.1),
        "hy_f_w2": nrm((N_ODD, HY_FILTER_HIDDEN, HY_FILTER_HIDDEN), HY_FILTER_HIDDEN ** -0.5),
        "hy_f_b2": nrm((N_ODD, HY_FILTER_HIDDEN), 0.1),
        "hy_f_w3": nrm((N_ODD, HY_FILTER_HIDDEN, HY_ORDER * 2 * HY_WIDTH), 0.005),
        "hy_f_freq": gain((N_ODD, 2, HY_FILTER_HIDDEN)),
        "hy_bias": nrm((N_ODD, HY_ORDER, HY_WIDTH), 0.5),
        "hy_out_w": nrm((N_ODD, HY_WIDTH, D_MODEL), HY_WIDTH ** -0.5),
        "final_g": gain((D_MODEL,)),
    }


def reference(x, c, cond_w, cond_b, mod_w, mod_b, norm_g, ffn_w1, ffn_w3, ffn_w2, ev_in_w, ev_out_w,
              ret_gn_g, lru_conv_w, lru_conv_b, lru_a_w, lru_a_b, lru_i_w, lru_i_b, lru_lambda,
              hy_in_w, hy_in_b, hy_conv_w, hy_conv_b, hy_f_w1, hy_f_b1, hy_f_w2, hy_f_b2, hy_f_w3,
              hy_f_freq, hy_bias, hy_out_w, final_g):
    B = x.shape[0]
    c_hid = jax.nn.silu(c @ cond_w + cond_b)
    for layer in range(DEPTH):
        mod = (c_hid @ mod_w[layer] + mod_b[layer]).reshape(B, N_MOD, D_MODEL)[:, :, None, :]
        h = rmsnorm(x, norm_g[layer, 0]) * (1.0 + mod[:, 1]) + mod[:, 0]
        x = x + 0.5 * (1.0 + mod[:, 2]) * swiglu(h, ffn_w1[layer, 0], ffn_w3[layer, 0], ffn_w2[layer, 0])
        h = rmsnorm(x, norm_g[layer, 1]) * (1.0 + mod[:, 4]) + mod[:, 3]
        if layer % 2 == 0:
            e = layer // 2
            mix = even_mixer(h, ev_in_w[e], ev_out_w[e], ret_gn_g[e], lru_conv_w[e], lru_conv_b[e],
                             lru_a_w[e], lru_a_b[e], lru_i_w[e], lru_i_b[e], lru_lambda[e])
        else:
            o = layer // 2
            mix = hyena_mixer(h, hy_in_w[o], hy_in_b[o], hy_conv_w[o], hy_conv_b[o], hy_f_w1[o], hy_f_b1[o],
                              hy_f_w2[o], hy_f_b2[o], hy_f_w3[o], hy_f_freq[o], hy_bias[o], hy_out_w[o])
        x = x + (1.0 + mod[:, 5]) * mix
        h = rmsnorm(x, norm_g[layer, 2]) * (1.0 + mod[:, 7]) + mod[:, 6]
        x = x + 0.5 * (1.0 + mod[:, 8]) * swiglu(h, ffn_w1[layer, 1], ffn_w3[layer, 1], ffn_w2[layer, 1])
    return rmsnorm(x, final_g)
```

```python
import functools
import math

import jax
import jax.numpy as jnp
from jax import lax
from jax.experimental import pallas as pl
from jax.experimental.pallas import tpu as pltpu

F32 = jnp.float32
BF16 = jnp.bfloat16

EPS = 1e-6
RET_HEADS = 8
RET_CHUNK = 256
ROPE_BASE = 10000.0
LRU_C = 8.0
LRU_ROWS = 128
HY_FAST_DECAY = 0.3
HY_SLOW_DECAY = 1.5
HY_TARGET = 1e-2
SUBLANES = 8
LANES = 128
HALO = SUBLANES
VMEM_LIMIT_BYTES = 56 * 1024 * 1024


def _params(n_axes, vmem=VMEM_LIMIT_BYTES):
    return pltpu.CompilerParams(dimension_semantics=("arbitrary",) * n_axes, vmem_limit_bytes=vmem)


def _bdot(a, b):
    return jnp.dot(a, b, preferred_element_type=F32)


def _norm_mod_kernel(x_ref, g_ref, sc_ref, sh_ref, o_ref):
    x = x_ref[...]
    y = x * lax.rsqrt(jnp.mean(x * x, axis=-1, keepdims=True) + EPS) * g_ref[...]
    o_ref[...] = (y * (1.0 + sc_ref[...]) + sh_ref[...]).astype(o_ref.dtype)


def _norm_kernel(x_ref, g_ref, o_ref):
    x = x_ref[...]
    y = x * lax.rsqrt(jnp.mean(x * x, axis=-1, keepdims=True) + EPS) * g_ref[...]
    o_ref[...] = y.astype(o_ref.dtype)


def _norm_mod(x, g, scale, shift, seq, tm=256):
    T, D = x.shape
    row = pl.BlockSpec((tm, D), lambda i: (i, 0))
    mod = pl.BlockSpec((None, 1, D), lambda i: ((i * tm) // seq, 0, 0))
    return pl.pallas_call(
        _norm_mod_kernel,
        grid=(T // tm,),
        in_specs=[row, pl.BlockSpec((1, D), lambda i: (0, 0)), mod, mod],
        out_specs=row,
        out_shape=jax.ShapeDtypeStruct((T, D), BF16),
        compiler_params=_params(1),
        name="norm_mod",
    )(x, g.reshape(1, D), scale, shift)


def _final_norm(x, g, tm=256):
    T, D = x.shape
    row = pl.BlockSpec((tm, D), lambda i: (i, 0))
    return pl.pallas_call(
        _norm_kernel,
        grid=(T // tm,),
        in_specs=[row, pl.BlockSpec((1, D), lambda i: (0, 0))],
        out_specs=row,
        out_shape=jax.ShapeDtypeStruct((T, D), x.dtype),
        compiler_params=_params(1),
        name="final_norm",
    )(x, g.reshape(1, D))


def _mm_kernel(*refs, has_bias):
    if has_bias:
        a_ref, w_ref, b_ref, o_ref = refs
    else:
        a_ref, w_ref, o_ref = refs
    acc = _bdot(a_ref[...].astype(BF16), w_ref[...].astype(BF16))
    if has_bias:
        acc = acc + b_ref[...]
    o_ref[...] = acc.astype(o_ref.dtype)


def _matmul(a, w, bias=None, *, out_dtype, tm=512, tn=512):
    M, K = a.shape
    N = w.shape[1]
    tm, tn = min(tm, M), min(tn, N)
    in_specs = [pl.BlockSpec((tm, K), lambda n, m: (m, 0)), pl.BlockSpec((K, tn), lambda n, m: (0, n))]
    args = [a, w]
    if bias is not None:
        in_specs.append(pl.BlockSpec((1, tn), lambda n, m: (0, n)))
        args.append(bias.reshape(1, N))
    return pl.pallas_call(
        functools.partial(_mm_kernel, has_bias=bias is not None),
        grid=(N // tn, M // tm),
        in_specs=in_specs,
        out_specs=pl.BlockSpec((tm, tn), lambda n, m: (m, n)),
        out_shape=jax.ShapeDtypeStruct((M, N), out_dtype),
        compiler_params=_params(2),
        name="matmul",
    )(*args)


def _ffn_up_kernel(h_ref, w1_ref, w3_ref, o_ref):
    h = h_ref[...]
    a = _bdot(h, w1_ref[...])
    b = _bdot(h, w3_ref[...])
    o_ref[...] = (a * jax.nn.sigmoid(a) * b).astype(o_ref.dtype)


def _ffn_up(h, w1, w3, tm=512, tn=512):
    M, K = h.shape
    N = w1.shape[1]
    wspec = pl.BlockSpec((K, tn), lambda n, m: (0, n))
    return pl.pallas_call(
        _ffn_up_kernel,
        grid=(N // tn, M // tm),
        in_specs=[pl.BlockSpec((tm, K), lambda n, m: (m, 0)), wspec, wspec],
        out_specs=pl.BlockSpec((tm, tn), lambda n, m: (m, n)),
        out_shape=jax.ShapeDtypeStruct((M, N), BF16),
        compiler_params=_params(2),
        name="ffn_up",
    )(h, w1, w3)


def _res_mm_kernel(*refs, n_pairs, coef):
    x_ref, g_ref, o_ref = refs[2 * n_pairs:]
    acc = _bdot(refs[0][...], refs[1][...])
    for i in range(1, n_pairs):
        acc = acc + _bdot(refs[2 * i][...], refs[2 * i + 1][...])
    o_ref[...] = x_ref[...] + (coef * (1.0 + g_ref[...])) * acc


def _res_mm(pairs, x, gate, seq, coef, tm=512, tn=512):
    T, N = x.shape
    in_specs, args = [], []
    for a, w in pairs:
        K = a.shape[1]
        in_specs += [pl.BlockSpec((tm, K), lambda n, m: (m, 0)), pl.BlockSpec((K, tn), lambda n, m: (0, n))]
        args += [a, w]
    tile = pl.BlockSpec((tm, tn), lambda n, m: (m, n))
    in_specs += [tile, pl.BlockSpec((None, 1, tn), lambda n, m: ((m * tm) // seq, 0, n))]
    return pl.pallas_call(
        functools.partial(_res_mm_kernel, n_pairs=len(pairs), coef=coef),
        grid=(N // tn, T // tm),
        in_specs=in_specs,
        out_specs=tile,
        out_shape=jax.ShapeDtypeStruct((T, N), x.dtype),
        compiler_params=_params(2),
        name="res_mm",
    )(*args, x, gate)


def _cond_kernel(c_ref, w_ref, b_ref, o_ref):
    a = _bdot(c_ref[...].astype(BF16), w_ref[...].astype(BF16)) + b_ref[...]
    o_ref[...] = a * jax.nn.sigmoid(a)


def _mod_kernel(h_ref, w_ref, b_ref, o_ref):
    o_ref[...] = _bdot(h_ref[...].astype(BF16), w_ref[...].astype(BF16)) + b_ref[...]


def _conditioning(c, cond_w, cond_b, mod_w, mod_b, tn=2048):
    B, D = c.shape
    tn = min(tn, D)
    H = cond_w.shape[1]
    depth, _, NM = mod_w.shape
    rows = SUBLANES
    c_pad = jnp.zeros((rows, D), c.dtype).at[:B].set(c)
    c_hid = pl.pallas_call(
        _cond_kernel,
        out_shape=jax.ShapeDtypeStruct((rows, H), F32),
        compiler_params=_params(0),
        name="cond",
    )(c_pad, cond_w, cond_b.reshape(1, H))
    mod = pl.pallas_call(
        _mod_kernel,
        grid=(depth, NM // tn),
        in_specs=[
            pl.BlockSpec((rows, H), lambda l, n: (0, 0)),
            pl.BlockSpec((None, H, tn), lambda l, n: (l, 0, n)),
            pl.BlockSpec((None, 1, tn), lambda l, n: (l, 0, n)),
        ],
        out_specs=pl.BlockSpec((None, rows, tn), lambda l, n: (l, 0, n)),
        out_shape=jax.ShapeDtypeStruct((depth, rows, NM), F32),
        compiler_params=_params(2),
        name="mod",
    )(c_hid, mod_w, mod_b.reshape(depth, 1, NM))
    return mod[:, :B].reshape(depth, B, NM // D, D)


def _retention_kernel(lg_ref, q_ref, k_ref, v_ref, g_ref, cos_ref, sin_ref, gn_ref, o_ref,
                      qr, kr, yacc, st, mask_s, xif_s, xib_s, zf_s, zb_s, *, chunk, k_scale):
    L, Dh = q_ref.shape
    half = Dh // 2
    C = chunk
    nc = L // C
    lg = lg_ref[pl.program_id(1)]
    contract_last = (((1,), (1,)), ((), ()))
    contract_rows = (((0,), (0,)), ((), ()))

    def rot_body(i, carry):
        r = pl.ds(pl.multiple_of(i * C, C), C)
        cs = cos_ref[r, :]
        sn = sin_ref[r, :]
        for src, dst, sc in ((q_ref, qr, 1.0), (k_ref, kr, k_scale)):
            x = src[r, :].astype(F32)
            x1 = x[:, :half]
            x2 = x[:, half:]
            y = jnp.concatenate([x1 * cs - x2 * sn, x1 * sn + x2 * cs], axis=-1) * sc
            dst[r, :] = y.astype(BF16)
        return carry

    lax.fori_loop(0, nc, rot_body, 0)

    n_i = lax.broadcasted_iota(jnp.int32, (C, C), 0)
    m_i = lax.broadcasted_iota(jnp.int32, (C, C), 1)
    mask_s[...] = jnp.exp(jnp.abs(n_i - m_i).astype(F32) * lg)
    row = lax.broadcasted_iota(jnp.int32, (C, Dh), 0).astype(F32)
    xif_s[...] = jnp.exp((row + 1.0) * lg)
    xib_s[...] = jnp.exp((C - row) * lg)
    zf_s[...] = jnp.exp((C - 1.0 - row) * lg)
    zb_s[...] = jnp.exp(row * lg)
    decay = jnp.exp(jnp.full((1, Dh), float(C), F32) * lg)

    def chunk_refs(c):
        r = pl.ds(pl.multiple_of(c * C, C), C)
        return r, qr[r, :], kr[r, :], v_ref[r, :]

    def advance(kc, vc, zeta):
        kz = (kc.astype(F32) * zeta).astype(BF16)
        st[...] = st[...] * decay + lax.dot_general(kz, vc, contract_rows, preferred_element_type=F32)

    st[...] = jnp.zeros_like(st)

    def fwd_body(c, carry):
        r, qc, kc, vc = chunk_refs(c)
        s = lax.dot_general(qc, kc, contract_last, preferred_element_type=F32) * mask_s[...]
        y = _bdot(s.astype(BF16), vc)
        y = y + _bdot(qc, st[...].astype(BF16)) * xif_s[...]
        yacc[r, :] = y
        advance(kc, vc, zf_s[...])
        return carry

    lax.fori_loop(0, nc, fwd_body, 0)

    st[...] = jnp.zeros_like(st)

    def bwd_body(i, carry):
        r, qc, kc, vc = chunk_refs(nc - 1 - i)
        y = yacc[r, :] + _bdot(qc, st[...].astype(BF16)) * xib_s[...]
        mu = jnp.mean(y, axis=-1, keepdims=True)
        d = y - mu
        var = jnp.mean(d * d, axis=-1, keepdims=True)
        yn = d * lax.rsqrt(var + EPS) * gn_ref[...]
        gg = g_ref[r, :].astype(F32)
        o_ref[r, :] = (gg * jax.nn.sigmoid(gg) * yn).astype(o_ref.dtype)
        advance(kc, vc, zb_s[...])
        return carry

    lax.fori_loop(0, nc, bwd_body, 0)


def _retention(u_ret, gn_g, batch, seq, n_heads, chunk=RET_CHUNK):
    T, W4 = u_ret.shape
    RW = W4 // 4
    Dh = RW // n_heads
    half = Dh // 2
    H = n_heads
    chunk = min(chunk, seq)
    log_g = jnp.log1p(-(2.0 ** (-5.0 - jnp.arange(H, dtype=F32))))
    inv = ROPE_BASE ** (-jnp.arange(half, dtype=F32) / half)
    ang = jnp.arange(seq, dtype=F32)[:, None] * inv[None, :]
    cos, sin = jnp.cos(ang), jnp.sin(ang)

    def col(off):
        return pl.BlockSpec((seq, Dh), lambda b, h: (b, off + h))

    table = pl.BlockSpec((seq, half), lambda b, h: (0, 0))
    return pl.pallas_call(
        functools.partial(_retention_kernel, chunk=chunk, k_scale=Dh ** -0.5),
        grid=(batch, H),
        in_specs=[pl.BlockSpec(memory_space=pltpu.SMEM), col(0), col(H), col(2 * H), col(3 * H), table, table,
                  pl.BlockSpec((1, Dh), lambda b, h: (0, h))],
        out_specs=pl.BlockSpec((seq, Dh), lambda b, h: (b, h)),
        out_shape=jax.ShapeDtypeStruct((T, RW), BF16),
        scratch_shapes=[
            pltpu.VMEM((seq, Dh), BF16), pltpu.VMEM((seq, Dh), BF16), pltpu.VMEM((seq, Dh), F32),
            pltpu.VMEM((Dh, Dh), F32), pltpu.VMEM((chunk, chunk), F32),
            pltpu.VMEM((chunk, Dh), F32), pltpu.VMEM((chunk, Dh), F32),
            pltpu.VMEM((chunk, Dh), F32), pltpu.VMEM((chunk, Dh), F32),
        ],
        compiler_params=_params(2),
        name="retention",
    )(log_g, u_ret, u_ret, u_ret, u_ret, cos, sin, gn_g.reshape(1, RW))


def _fill_padded(src_ref, xp, rows):
    L, n = src_ref.shape
    zeros = jnp.zeros((HALO, n), xp.dtype)
    xp[0:HALO, :] = zeros
    xp[HALO + L:HALO + L + HALO, :] = zeros

    def body(i, carry):
        r0 = pl.multiple_of(i * rows, rows)
        xp[pl.ds(r0 + HALO, rows), :] = src_ref[pl.ds(r0, rows), :].astype(xp.dtype)
        return carry

    lax.fori_loop(0, L // rows, body, 0)


def _taps(xe, w_ref, b_ref, rows, first):
    acc = b_ref[...]
    for k in range(w_ref.shape[0]):
        o = HALO + first + k
        acc = acc + w_ref[k:k + 1, :] * xe[o:o + rows, :]
    return acc


def _lru_kernel(xr_ref, xg_ref, cw_ref, cb_ref, w_ref, b_ref, lam_ref, o_ref, xp, hf, *, rows):
    L, n = xr_ref.shape
    R = rows
    nc = L // R
    groups = R // SUBLANES
    _fill_padded(xr_ref, xp, R)
    nl = -lam_ref[...]
    c8 = -LRU_C * (jnp.maximum(nl, 0.0) + jnp.log1p(jnp.exp(-jnp.abs(nl))))
    rowmod = lax.broadcasted_iota(jnp.int32, (R, n), 0) & (SUBLANES - 1)

    def gates(r0, d):
        xe = xp[pl.ds(r0, R + 2 * HALO), :]
        xc = _taps(xe, cw_ref, cb_ref, R, -2)
        pre = _bdot(xc.astype(BF16), w_ref[:, 2 * n * d:2 * n * (d + 1)]) + b_ref[:, 2 * n * d:2 * n * (d + 1)]
        r = jax.nn.sigmoid(pre[:, :n])
        ig = jax.nn.sigmoid(pre[:, n:])
        log_a = c8[d:d + 1, :] * r
        a = jnp.exp(log_a)
        bb = jnp.sqrt(-jnp.tanh(log_a) * (a * a + 1.0)) * (ig * xc)
        return a, bb

    def scan_chunk(a, bb, carry, reverse):
        for s in (1, 2, 4):
            if reverse:
                shift, ok = R - s, rowmod < SUBLANES - s
            else:
                shift, ok = s, rowmod >= s
            a_sh = pltpu.roll(a, shift, 0)
            b_sh = pltpu.roll(bb, shift, 0)
            bb = jnp.where(ok, a * b_sh + bb, bb)
            a = jnp.where(ok, a * a_sh, a)
        out = [None] * groups
        order = range(groups - 1, -1, -1) if reverse else range(groups)
        for gi in order:
            sl = slice(gi * SUBLANES, (gi + 1) * SUBLANES)
            hg = bb[sl, :] + a[sl, :] * carry
            out[gi] = hg
            carry = hg[0:1, :] if reverse else hg[SUBLANES - 1:SUBLANES, :]
        return jnp.concatenate(out, axis=0), carry

    def fwd_body(i, carry):
        r0 = pl.multiple_of(i * R, R)
        a, bb = gates(r0, 0)
        h, carry = scan_chunk(a, bb, carry, False)
        hf[pl.ds(r0, R), :] = h
        return carry

    lax.fori_loop(0, nc, fwd_body, jnp.zeros((1, n), F32))

    def bwd_body(i, carry):
        r0 = pl.multiple_of((nc - 1 - i) * R, R)
        a, bb = gates(r0, 1)
        h, carry = scan_chunk(a, bb, carry, True)
        r = pl.ds(r0, R)
        xg = xg_ref[r, :]
        gelu = 0.5 * xg * (1.0 + jnp.tanh(math.sqrt(2.0 / math.pi) * (xg + 0.044715 * (xg * xg * xg))))
        o_ref[r, :] = ((hf[r, :] + h) * gelu).astype(o_ref.dtype)
        return carry

    lax.fori_loop(0, nc, bwd_body, jnp.zeros((1, n), F32))


def _lru(u_lru, conv_w, conv_b, a_w, a_b, i_w, i_b, lam, batch, seq, rows=LRU_ROWS):
    T, W2 = u_lru.shape
    W = W2 // 2
    nb, n = a_w.shape[1], a_w.shape[2]
    rows = min(rows, seq)
    w_cat = jnp.concatenate([a_w[0], i_w[0], a_w[1], i_w[1]], axis=-1).astype(BF16)
    b_cat = jnp.stack([a_b[0], i_b[0], a_b[1], i_b[1]], axis=0).reshape(4, nb, n).transpose(1, 0, 2).reshape(nb, 1, 4 * n)
    return pl.pallas_call(
        functools.partial(_lru_kernel, rows=rows),
        grid=(batch, nb),
        in_specs=[
            pl.BlockSpec((seq, n), lambda b, j: (b, j)),
            pl.BlockSpec((seq, n), lambda b, j: (b, nb + j)),
            pl.BlockSpec((conv_w.shape[0], n), lambda b, j: (0, j)),
            pl.BlockSpec((1, n), lambda b, j: (0, j)),
            pl.BlockSpec((None, n, 4 * n), lambda b, j: (j, 0, 0)),
            pl.BlockSpec((None, 1, 4 * n), lambda b, j: (j, 0, 0)),
            pl.BlockSpec((2, n), lambda b, j: (0, j)),
        ],
        out_specs=pl.BlockSpec((seq, n), lambda b, j: (b, j)),
        out_shape=jax.ShapeDtypeStruct((T, W), BF16),
        scratch_shapes=[pltpu.VMEM((seq + 2 * HALO, n), F32), pltpu.VMEM((seq, n), F32)],
        compiler_params=_params(2),
        name="rglru",
    )(u_lru, u_lru, conv_w, conv_b.reshape(1, W), w_cat, b_cat, lam)


def _conv3_kernel(u_ref, w_ref, b_ref, o_ref, xp, *, rows):
    L, n = u_ref.shape
    _fill_padded(u_ref, xp, rows)

    def body(i, carry):
        r0 = pl.multiple_of(i * rows, rows)
        xe = xp[pl.ds(r0, rows + 2 * HALO), :]
        o_ref[pl.ds(r0, rows), :] = _taps(xe, w_ref, b_ref, rows, -1).astype(o_ref.dtype)
        return carry

    lax.fori_loop(0, L // rows, body, 0)


def _conv3(u, w, b, batch, seq, tn=256, rows=256):
    T, N = u.shape
    rows = min(rows, seq)
    blk = pl.BlockSpec((seq, tn), lambda bi, j: (bi, j))
    return pl.pallas_call(
        functools.partial(_conv3_kernel, rows=rows),
        grid=(batch, N // tn),
        in_specs=[blk, pl.BlockSpec((w.shape[0], tn), lambda bi, j: (0, j)), pl.BlockSpec((1, tn), lambda bi, j: (0, j))],
        out_specs=blk,
        out_shape=jax.ShapeDtypeStruct((T, N), BF16),
        scratch_shapes=[pltpu.VMEM((seq + 2 * HALO, tn), F32)],
        compiler_params=_params(2),
        name="conv3",
    )(u, w, b.reshape(1, N))


def _filter_kernel(z_ref, w1_ref, b1_ref, w2_ref, b2_ref, fr_ref, wf_ref, wb_ref, dl_ref, e_ref, o_ref, hid, *, rows):
    L = z_ref.shape[0]
    hi = lax.Precision.HIGHEST

    @pl.when((pl.program_id(0) == 0) & (pl.program_id(1) == 0))
    def _():
        def body(i, carry):
            r = pl.ds(pl.multiple_of(i * rows, rows), rows)
            h1 = jnp.sin(fr_ref[0:1, :] * (jnp.dot(z_ref[r, :], w1_ref[...], precision=hi, preferred_element_type=F32) + b1_ref[...]))
            h2 = jnp.sin(fr_ref[1:2, :] * (jnp.dot(h1, w2_ref[...], precision=hi, preferred_element_type=F32) + b2_ref[...]))
            hid[r, :] = h2
            return carry

        lax.fori_loop(0, L // rows, body, 0)

    def body(i, carry):
        r0 = pl.multiple_of(i * rows, rows)
        r = pl.ds(r0, rows)
        hh = hid[r, :].astype(BF16)
        t = z_ref[r, 0:1]
        win = jnp.exp(-t * jnp.abs(dl_ref[...]))
        hf = _bdot(hh, wf_ref[...].astype(BF16)) * win
        hb = _bdot(hh, wb_ref[...].astype(BF16)) * win
        lag = lax.broadcasted_iota(jnp.int32, hb.shape, 0) + r0
        hb = jnp.where(lag == 0, 0.0, hb)
        e_ref[r, :] = (hf + hb).astype(e_ref.dtype)
        o_ref[r, :] = (hf - hb).astype(o_ref.dtype)
        return carry

    lax.fori_loop(0, L // rows, body, 0)


def _hyena_filters(seq, width, w1, b1, w2, b2, w3, freq, tn=512, rows=512):
    emb, hid_in = w1.shape
    order = w3.shape[1] // (2 * width)
    rows = min(rows, seq)
    tn = min(tn, width)
    nt = width // tn
    t = jnp.linspace(0.0, 1.0, seq, dtype=F32)[:, None]
    bands = (emb - 1) // 2
    f = jnp.linspace(1e-4, bands - 1, bands, dtype=F32)[None, :]
    w = 2.0 * math.pi * jnp.arange(seq, dtype=F32)[:, None] / seq
    z = jnp.concatenate([t, jnp.cos(f * w), -jnp.sin(f * w)], axis=-1)
    hidden = LANES
    pad = lambda a, shape: jnp.zeros(shape, F32).at[tuple(slice(0, d) for d in a.shape)].set(a)
    zp = pad(z, (seq, LANES))
    w1p = pad(w1, (LANES, hidden))
    b1, b2 = pad(b1[None], (1, hidden)), pad(b2[None], (1, hidden))
    w2 = pad(w2, (hidden, hidden))
    w3 = pad(w3, (hidden, w3.shape[1]))
    freq = pad(freq, (2, hidden))
    deltas = jnp.linspace(math.log(HY_TARGET) / HY_SLOW_DECAY, math.log(HY_TARGET) / HY_FAST_DECAY, width, dtype=F32)
    full = lambda shape: pl.BlockSpec(shape, lambda o, j: (0,) * len(shape))
    out = pl.BlockSpec((None, seq, tn), lambda o, j: (o, 0, j))
    return pl.pallas_call(
        functools.partial(_filter_kernel, rows=rows),
        grid=(order, nt),
        in_specs=[
            full((seq, LANES)), full((LANES, hidden)), full((1, hidden)), full((hidden, hidden)), full((1, hidden)),
            full((2, hidden)),
            pl.BlockSpec((hidden, tn), lambda o, j: (0, o * 2 * nt + j)),
            pl.BlockSpec((hidden, tn), lambda o, j: (0, o * 2 * nt + nt + j)),
            pl.BlockSpec((1, tn), lambda o, j: (0, j)),
        ],
        out_specs=[out, out],
        out_shape=[jax.ShapeDtypeStruct((order, seq, width), BF16)] * 2,
        scratch_shapes=[pltpu.VMEM((seq, hidden), F32)],
        compiler_params=_params(2),
        name="hyena_filter",
    )(zp, w1p, b1.reshape(1, hidden), w2, b2.reshape(1, hidden), freq, w3, w3, deltas.reshape(1, width))


def _dft_matrices(seq):
    i = jnp.arange(seq, dtype=jnp.int32)
    period = 4 * seq
    m_kt = ((2 * i[:, None] + 1) * i[None, :]) & (period - 1)
    ang = m_kt.astype(F32) * (2.0 * math.pi / period)
    ang_t = m_kt.T.astype(F32) * (2.0 * math.pi / period)
    return jnp.cos(ang).astype(BF16), jnp.sin(ang).astype(BF16), jnp.cos(ang_t).astype(BF16), jnp.sin(ang_t).astype(BF16)


def _spec_kernel(c_ref, s_ref, e_ref, o_ref, ka_ref, kb_ref):
    ka_ref[...] = _bdot(c_ref[...], e_ref[...])
    kb_ref[...] = _bdot(s_ref[...], o_ref[...])


def _filter_spectrum(cm, sm, e, o, tk=512, tn=512):
    order, L, W = e.shape
    tk, tn = min(tk, L), min(tn, W)
    mat = pl.BlockSpec((tk, L), lambda i, q, j: (i, 0))
    x = pl.BlockSpec((None, L, tn), lambda i, q, j: (q, 0, j))
    out = pl.BlockSpec((None, tk, tn), lambda i, q, j: (q, i, j))
    return pl.pallas_call(
        _spec_kernel,
        grid=(L // tk, order, W // tn),
        in_specs=[mat, mat, x, x],
        out_specs=[out, out],
        out_shape=[jax.ShapeDtypeStruct((order, L, W), F32)] * 2,
        compiler_params=_params(3),
        name="hyena_spectrum",
    )(cm, sm, e, o)


def _dft_fwd_kernel(c_ref, s_ref, x_ref, ka_ref, kb_ref, pa_ref, pb_ref, *, scale):
    x = x_ref[...]
    a = _bdot(c_ref[...], x)
    b = _bdot(s_ref[...], x)
    ka = ka_ref[...] * scale
    kb = kb_ref[...] * scale
    pa_ref[...] = (a * ka - b * kb).astype(pa_ref.dtype)
    pb_ref[...] = (a * kb + b * ka).astype(pb_ref.dtype)


def _dft_fwd(cm, sm, x, x_col, ka, kb, order, tk=512, tn=512):
    B, L, _ = x.shape
    W = ka.shape[2]
    tk, tn = min(tk, L), min(tn, W)
    xo = x_col // tn
    mat = pl.BlockSpec((tk, L), lambda i, b, j: (i, 0))
    kspec = pl.BlockSpec((None, tk, tn), lambda i, b, j: (order, i, j))
    out = pl.BlockSpec((None, tk, tn), lambda i, b, j: (b, i, j))
    return pl.pallas_call(
        functools.partial(_dft_fwd_kernel, scale=1.0 / L),
        grid=(L // tk, B, W // tn),
        in_specs=[mat, mat, pl.BlockSpec((None, L, tn), lambda i, b, j: (b, 0, xo + j)), kspec, kspec],
        out_specs=[out, out],
        out_shape=[jax.ShapeDtypeStruct((B, L, W), BF16)] * 2,
        compiler_params=_params(3),
        name="hyena_dft_fwd",
    )(cm, sm, x, ka, kb)


def _dft_inv_kernel(c_ref, s_ref, pa_ref, pb_ref, x_ref, g_ref, bias_ref, o_ref):
    y = _bdot(c_ref[...], pa_ref[...]) + _bdot(s_ref[...], pb_ref[...])
    x = x_ref[...].astype(F32)
    o_ref[...] = (g_ref[...].astype(F32) * (y + bias_ref[...] * x)).astype(o_ref.dtype)


def _dft_inv(cmt, smt, pa, pb, x, x_col, gate, gate_col, bias, tt=512, tn=512):
    B, L, W = pa.shape
    tt, tn = min(tt, L), min(tn, W)
    xo, go = x_col // tn, gate_col // tn
    mat = pl.BlockSpec((tt, L), lambda i, b, j: (i, 0))
    spec = pl.BlockSpec((None, L, tn), lambda i, b, j: (b, 0, j))
    tile = pl.BlockSpec((None, tt, tn), lambda i, b, j: (b, i, j))
    return pl.pallas_call(
        _dft_inv_kernel,
        grid=(L // tt, B, W // tn),
        in_specs=[mat, mat, spec, spec,
                  pl.BlockSpec((None, tt, tn), lambda i, b, j: (b, i, xo + j)),
                  pl.BlockSpec((None, tt, tn), lambda i, b, j: (b, i, go + j)),
                  pl.BlockSpec((1, tn), lambda i, b, j: (0, j))],
        out_specs=tile,
        out_shape=jax.ShapeDtypeStruct((B, L, W), BF16),
        compiler_params=_params(3),
        name="hyena_dft_inv",
    )(cmt, smt, pa, pb, x, gate, bias.reshape(1, W))


def _hyena_mix(h, dft, w_in, b_in, conv_w, conv_b, f_w1, f_b1, f_w2, f_b2, f_w3, f_freq, bias, batch, seq):
    T = h.shape[0]
    W = w_in.shape[1] // 3
    cm, sm, cmt, smt = dft
    u = _matmul(h, w_in.astype(BF16), b_in, out_dtype=F32)
    uc = _conv3(u, conv_w, conv_b, batch, seq).reshape(batch, seq, 3 * W)
    e, o = _hyena_filters(seq, W, f_w1, f_b1, f_w2, f_b2, f_w3, f_freq)
    ka, kb = _filter_spectrum(cm, sm, e, o)
    z = uc
    for order in range(ka.shape[0]):
        pa, pb = _dft_fwd(cm, sm, z, 0, ka, kb, order)
        z = _dft_inv(cmt, smt, pa, pb, z, 0, uc, (order + 1) * W, bias[order])
    return z.reshape(T, W)


def kernel(x, c, cond_w, cond_b, mod_w, mod_b, norm_g, ffn_w1, ffn_w3, ffn_w2, ev_in_w, ev_out_w, ret_gn_g, lru_conv_w, lru_conv_b, lru_a_w, lru_a_b, lru_i_w, lru_i_b, lru_lambda, hy_in_w, hy_in_b, hy_conv_w, hy_conv_b, hy_f_w1, hy_f_b1, hy_f_w2, hy_f_b2, hy_f_w3, hy_f_freq, hy_bias, hy_out_w, final_g):
    B, L, D = x.shape
    depth = mod_w.shape[0]
    T = B * L
    mods = _conditioning(c, cond_w, cond_b, mod_w, mod_b)
    dft = _dft_matrices(L) if depth > 1 else None
    xt = x.reshape(T, D)

    def ffn(xt, layer, which, mod, idx):
        shift, scale, gate = (mod[:, j:j + 1, :] for j in idx)
        h = _norm_mod(xt, norm_g[layer, 2 * which], scale, shift, L)
        u = _ffn_up(h, ffn_w1[layer, which].astype(BF16), ffn_w3[layer, which].astype(BF16))
        return _res_mm([(u, ffn_w2[layer, which].astype(BF16))], xt, gate, L, 0.5)

    for layer in range(depth):
        mod = mods[layer]
        xt = ffn(xt, layer, 0, mod, (0, 1, 2))
        h = _norm_mod(xt, norm_g[layer, 1], mod[:, 4:5, :], mod[:, 3:4, :], L)
        gate = mod[:, 5:6, :]
        if layer % 2 == 0:
            e = layer // 2
            RW = ret_gn_g.shape[1]
            w_in = ev_in_w[e]
            u_ret = _matmul(h, w_in[:, :4 * RW].astype(BF16), out_dtype=BF16)
            u_lru = _matmul(h, w_in[:, 4 * RW:].astype(BF16), out_dtype=F32)
            ret = _retention(u_ret, ret_gn_g[e], B, L, RET_HEADS)
            lru = _lru(u_lru, lru_conv_w[e], lru_conv_b[e], lru_a_w[e], lru_a_b[e], lru_i_w[e], lru_i_b[e],
                       lru_lambda[e], B, L)
            w_out = ev_out_w[e].astype(BF16)
            xt = _res_mm([(ret, w_out[:RW]), (lru, w_out[RW:])], xt, gate, L, 1.0)
        else:
            o = layer // 2
            z = _hyena_mix(h, dft, hy_in_w[o], hy_in_b[o], hy_conv_w[o], hy_conv_b[o], hy_f_w1[o], hy_f_b1[o],
                           hy_f_w2[o], hy_f_b2[o], hy_f_w3[o], hy_f_freq[o], hy_bias[o], B, L)
            xt = _res_mm([(z, hy_out_w[o].astype(BF16))], xt, gate, L, 1.0)
        xt = ffn(xt, layer, 1, mod, (6, 7, 8))
    return _final_norm(xt, final_g).reshape(B, L, D)
```

```python
import functools
import math

import jax
import jax.numpy as jnp
from jax import lax
from jax.experimental import pallas as pl
from jax.experimental.pallas import tpu as pltpu

F32 = jnp.float32
BF16 = jnp.bfloat16

EPS = 1e-6
RET_HEADS = 8
RET_CHUNK = 256
ROPE_BASE = 10000.0
LRU_C = 8.0
LRU_ROWS = 128
HY_FAST_DECAY = 0.3
HY_SLOW_DECAY = 1.5
HY_TARGET = 1e-2
SUBLANES = 8
LANES = 128
HALO = SUBLANES
VMEM_LIMIT_BYTES = 62 * 1024 * 1024


def _params(n_axes, vmem=VMEM_LIMIT_BYTES):
    return pltpu.CompilerParams(dimension_semantics=("arbitrary",) * n_axes, vmem_limit_bytes=vmem)


def _bdot(a, b):
    return jnp.dot(a, b, preferred_element_type=F32)


def _norm_mod_kernel(x_ref, g_ref, sc_ref, sh_ref, o_ref):
    x = x_ref[...]
    y = x * lax.rsqrt(jnp.mean(x * x, axis=-1, keepdims=True) + EPS) * g_ref[...]
    o_ref[...] = (y * (1.0 + sc_ref[...]) + sh_ref[...]).astype(o_ref.dtype)


def _norm_kernel(x_ref, g_ref, o_ref):
    x = x_ref[...]
    y = x * lax.rsqrt(jnp.mean(x * x, axis=-1, keepdims=True) + EPS) * g_ref[...]
    o_ref[...] = y.astype(o_ref.dtype)


def _norm_mod(x, g, scale, shift, seq, tm=256):
    T, D = x.shape
    row = pl.BlockSpec((tm, D), lambda i: (i, 0))
    mod = pl.BlockSpec((None, 1, D), lambda i: ((i * tm) // seq, 0, 0))
    return pl.pallas_call(
        _norm_mod_kernel,
        grid=(T // tm,),
        in_specs=[row, pl.BlockSpec((1, D), lambda i: (0, 0)), mod, mod],
        out_specs=row,
        out_shape=jax.ShapeDtypeStruct((T, D), BF16),
        compiler_params=_params(1),
        name="norm_mod",
    )(x, g.reshape(1, D), scale, shift)


def _final_norm(x, g, tm=256):
    T, D = x.shape
    row = pl.BlockSpec((tm, D), lambda i: (i, 0))
    return pl.pallas_call(
        _norm_kernel,
        grid=(T // tm,),
        in_specs=[row, pl.BlockSpec((1, D), lambda i: (0, 0))],
        out_specs=row,
        out_shape=jax.ShapeDtypeStruct((T, D), x.dtype),
        compiler_params=_params(1),
        name="final_norm",
    )(x, g.reshape(1, D))


def _resident(tm, K):
    return pl.BlockSpec((tm, K), lambda m, n: (m, 0), pipeline_mode=pl.Buffered(1))


def _mm_kernel(*refs, has_bias):
    if has_bias:
        a_ref, w_ref, b_ref, o_ref = refs
    else:
        a_ref, w_ref, o_ref = refs
    acc = _bdot(a_ref[...], w_ref[...].astype(BF16))
    if has_bias:
        acc = acc + b_ref[...]
    o_ref[...] = acc.astype(o_ref.dtype)


def _matmul(a, w, bias=None, *, w_col=0, n_out=None, out_dtype, tm=2048, tn=512):
    M, K = a.shape
    N = w.shape[1] - w_col if n_out is None else n_out
    tm, tn = min(tm, M), min(tn, N)
    co = w_col // tn
    in_specs = [_resident(tm, K), pl.BlockSpec((K, tn), lambda m, n: (0, co + n))]
    args = [a, w]
    if bias is not None:
        in_specs.append(pl.BlockSpec((1, tn), lambda m, n: (0, n)))
        args.append(bias.reshape(1, N))
    return pl.pallas_call(
        functools.partial(_mm_kernel, has_bias=bias is not None),
        grid=(M // tm, N // tn),
        in_specs=in_specs,
        out_specs=pl.BlockSpec((tm, tn), lambda m, n: (m, n)),
        out_shape=jax.ShapeDtypeStruct((M, N), out_dtype),
        compiler_params=_params(2),
        name="matmul",
    )(*args)


def _ffn_up_kernel(h_ref, w1_ref, w3_ref, o_ref):
    h = h_ref[...]
    a = _bdot(h, w1_ref[...].astype(BF16))
    b = _bdot(h, w3_ref[...].astype(BF16))
    o_ref[...] = (a * jax.nn.sigmoid(a) * b).astype(o_ref.dtype)


def _ffn_up(h, w1, w3, tm=2048, tn=256):
    M, K = h.shape
    N = w1.shape[1]
    tm = min(tm, M)
    wspec = pl.BlockSpec((K, tn), lambda m, n: (0, n))
    return pl.pallas_call(
        _ffn_up_kernel,
        grid=(M // tm, N // tn),
        in_specs=[_resident(tm, K), wspec, wspec],
        out_specs=pl.BlockSpec((tm, tn), lambda m, n: (m, n)),
        out_shape=jax.ShapeDtypeStruct((M, N), BF16),
        compiler_params=_params(2),
        name="ffn_up",
    )(h, w1, w3)


def _res_mm_kernel(*refs, n_pairs, coef):
    x_ref, g_ref, o_ref = refs[2 * n_pairs:]
    acc = _bdot(refs[0][...], refs[1][...].astype(BF16))
    for i in range(1, n_pairs):
        acc = acc + _bdot(refs[2 * i][...], refs[2 * i + 1][...].astype(BF16))
    o_ref[...] = x_ref[...] + (coef * (1.0 + g_ref[...])) * acc


def _res_mm(acts, w, x, gate, seq, coef, tm=2048, tn=256):
    T, N = x.shape
    tm = min(tm, T, seq)
    in_specs = []
    for i, a in enumerate(acts):
        K = a.shape[1]
        in_specs += [_resident(tm, K), pl.BlockSpec((K, tn), lambda m, n, i=i: (i, n))]
    tile = pl.BlockSpec((tm, tn), lambda m, n: (m, n))
    in_specs += [tile, pl.BlockSpec((None, 1, tn), lambda m, n: ((m * tm) // seq, 0, n))]
    args = [v for a in acts for v in (a, w)]
    return pl.pallas_call(
        functools.partial(_res_mm_kernel, n_pairs=len(acts), coef=coef),
        grid=(T // tm, N // tn),
        in_specs=in_specs,
        out_specs=tile,
        out_shape=jax.ShapeDtypeStruct((T, N), x.dtype),
        compiler_params=_params(2),
        name="res_mm",
    )(*args, x, gate)


def _cond_kernel(c_ref, w_ref, b_ref, o_ref):
    a = _bdot(c_ref[...].astype(BF16), w_ref[...].astype(BF16)) + b_ref[...]
    o_ref[...] = a * jax.nn.sigmoid(a)


def _mod_kernel(h_ref, w_ref, b_ref, o_ref):
    o_ref[...] = _bdot(h_ref[...].astype(BF16), w_ref[...].astype(BF16)) + b_ref[...]


def _conditioning(c, cond_w, cond_b, mod_w, mod_b, tn=2048):
    B, D = c.shape
    tn = min(tn, D)
    H = cond_w.shape[1]
    depth, _, NM = mod_w.shape
    rows = SUBLANES
    c_pad = jnp.zeros((rows, D), c.dtype).at[:B].set(c)
    c_hid = pl.pallas_call(
        _cond_kernel,
        out_shape=jax.ShapeDtypeStruct((rows, H), F32),
        compiler_params=_params(0),
        name="cond",
    )(c_pad, cond_w, cond_b.reshape(1, H))
    mod = pl.pallas_call(
        _mod_kernel,
        grid=(depth, NM // tn),
        in_specs=[
            pl.BlockSpec((rows, H), lambda l, n: (0, 0)),
            pl.BlockSpec((None, H, tn), lambda l, n: (l, 0, n)),
            pl.BlockSpec((None, 1, tn), lambda l, n: (l, 0, n)),
        ],
        out_specs=pl.BlockSpec((None, rows, tn), lambda l, n: (l, 0, n)),
        out_shape=jax.ShapeDtypeStruct((depth, rows, NM), F32),
        compiler_params=_params(2),
        name="mod",
    )(c_hid, mod_w, mod_b.reshape(depth, 1, NM))
    return mod[:, :B].reshape(depth, B, NM // D, D)


def _retention_kernel(lg_ref, q_ref, k_ref, v_ref, g_ref, cos_ref, sin_ref, gn_ref, o_ref,
                      qr, kr, yacc, st, mask_s, xif_s, xib_s, zf_s, zb_s, *, chunk, k_scale):
    L, Dh = q_ref.shape
    half = Dh // 2
    C = chunk
    nc = L // C
    lg = lg_ref[pl.program_id(1)]
    contract_last = (((1,), (1,)), ((), ()))
    contract_rows = (((0,), (0,)), ((), ()))

    def rot_body(i, carry):
        r = pl.ds(pl.multiple_of(i * C, C), C)
        cs = cos_ref[r, :]
        sn = sin_ref[r, :]
        for src, dst, sc in ((q_ref, qr, 1.0), (k_ref, kr, k_scale)):
            x = src[r, :].astype(F32)
            x1 = x[:, :half]
            x2 = x[:, half:]
            y = jnp.concatenate([x1 * cs - x2 * sn, x1 * sn + x2 * cs], axis=-1) * sc
            dst[r, :] = y.astype(BF16)
        return carry

    lax.fori_loop(0, nc, rot_body, 0)

    n_i = lax.broadcasted_iota(jnp.int32, (C, C), 0)
    m_i = lax.broadcasted_iota(jnp.int32, (C, C), 1)
    mask_s[...] = jnp.exp(jnp.abs(n_i - m_i).astype(F32) * lg)
    row = lax.broadcasted_iota(jnp.int32, (C, Dh), 0).astype(F32)
    xif_s[...] = jnp.exp((row + 1.0) * lg)
    xib_s[...] = jnp.exp((C - row) * lg)
    zf_s[...] = jnp.exp((C - 1.0 - row) * lg)
    zb_s[...] = jnp.exp(row * lg)
    decay = jnp.exp(jnp.full((1, Dh), float(C), F32) * lg)

    def chunk_refs(c):
        r = pl.ds(pl.multiple_of(c * C, C), C)
        return r, qr[r, :], kr[r, :], v_ref[r, :]

    def advance(kc, vc, zeta):
        kz = (kc.astype(F32) * zeta).astype(BF16)
        st[...] = st[...] * decay + lax.dot_general(kz, vc, contract_rows, preferred_element_type=F32)

    st[...] = jnp.zeros_like(st)

    def fwd_body(c, carry):
        r, qc, kc, vc = chunk_refs(c)
        s = lax.dot_general(qc, kc, contract_last, preferred_element_type=F32) * mask_s[...]
        y = _bdot(s.astype(BF16), vc)
        y = y + _bdot(qc, st[...].astype(BF16)) * xif_s[...]
        yacc[r, :] = y
        advance(kc, vc, zf_s[...])
        return carry

    lax.fori_loop(0, nc, fwd_body, 0)

    st[...] = jnp.zeros_like(st)

    def bwd_body(i, carry):
        r, qc, kc, vc = chunk_refs(nc - 1 - i)
        y = yacc[r, :] + _bdot(qc, st[...].astype(BF16)) * xib_s[...]
        mu = jnp.mean(y, axis=-1, keepdims=True)
        d = y - mu
        var = jnp.mean(d * d, axis=-1, keepdims=True)
        yn = d * lax.rsqrt(var + EPS) * gn_ref[...]
        gg = g_ref[r, :].astype(F32)
        o_ref[r, :] = (gg * jax.nn.sigmoid(gg) * yn).astype(o_ref.dtype)
        advance(kc, vc, zb_s[...])
        return carry

    lax.fori_loop(0, nc, bwd_body, 0)


def _retention(u_ret, gn_g, batch, seq, n_heads, chunk=RET_CHUNK):
    T, W4 = u_ret.shape
    RW = W4 // 4
    Dh = RW // n_heads
    half = Dh // 2
    H = n_heads
    chunk = min(chunk, seq)
    log_g = jnp.log1p(-(2.0 ** (-5.0 - jnp.arange(H, dtype=F32))))
    inv = ROPE_BASE ** (-jnp.arange(half, dtype=F32) / half)
    ang = jnp.arange(seq, dtype=F32)[:, None] * inv[None, :]
    cos, sin = jnp.cos(ang), jnp.sin(ang)

    def col(off):
        return pl.BlockSpec((seq, Dh), lambda b, h: (b, off + h))

    table = pl.BlockSpec((seq, half), lambda b, h: (0, 0))
    return pl.pallas_call(
        functools.partial(_retention_kernel, chunk=chunk, k_scale=Dh ** -0.5),
        grid=(batch, H),
        in_specs=[pl.BlockSpec(memory_space=pltpu.SMEM), col(0), col(H), col(2 * H), col(3 * H), table, table,
                  pl.BlockSpec((1, Dh), lambda b, h: (0, h))],
        out_specs=pl.BlockSpec((seq, Dh), lambda b, h: (b, h)),
        out_shape=jax.ShapeDtypeStruct((T, RW), BF16),
        scratch_shapes=[
            pltpu.VMEM((seq, Dh), BF16), pltpu.VMEM((seq, Dh), BF16), pltpu.VMEM((seq, Dh), F32),
            pltpu.VMEM((Dh, Dh), F32), pltpu.VMEM((chunk, chunk), F32),
            pltpu.VMEM((chunk, Dh), F32), pltpu.VMEM((chunk, Dh), F32),
            pltpu.VMEM((chunk, Dh), F32), pltpu.VMEM((chunk, Dh), F32),
        ],
        compiler_params=_params(2),
        name="retention",
    )(log_g, u_ret, u_ret, u_ret, u_ret, cos, sin, gn_g.reshape(1, RW))


def _fill_padded(src_ref, xp, rows):
    L, n = src_ref.shape
    zeros = jnp.zeros((HALO, n), xp.dtype)
    xp[0:HALO, :] = zeros
    xp[HALO + L:HALO + L + HALO, :] = zeros

    def body(i, carry):
        r0 = pl.multiple_of(i * rows, rows)
        xp[pl.ds(r0 + HALO, rows), :] = src_ref[pl.ds(r0, rows), :].astype(xp.dtype)
        return carry

    lax.fori_loop(0, L // rows, body, 0)


def _taps(xe, w_ref, b_ref, rows, first):
    acc = b_ref[...]
    for k in range(w_ref.shape[0]):
        o = HALO + first + k
        acc = acc + w_ref[k:k + 1, :] * xe[o:o + rows, :]
    return acc


def _lru_kernel(xr_ref, xg_ref, cw_ref, cb_ref, w_ref, b_ref, lam_ref, o_ref, xp, hf, hb, *, rows):
    L, n = xr_ref.shape
    R = rows
    nc = L // R
    groups = R // SUBLANES
    _fill_padded(xr_ref, xp, R)
    nl = -lam_ref[...]
    c8 = -LRU_C * (jnp.maximum(nl, 0.0) + jnp.log1p(jnp.exp(-jnp.abs(nl))))
    rowmod = lax.broadcasted_iota(jnp.int32, (R, n), 0) & (SUBLANES - 1)

    def gates(r0, d):
        xe = xp[pl.ds(r0, R + 2 * HALO), :]
        xc = _taps(xe, cw_ref, cb_ref, R, -2)
        pre = _bdot(xc.astype(BF16), w_ref[:, 2 * n * d:2 * n * (d + 1)]) + b_ref[:, 2 * n * d:2 * n * (d + 1)]
        r = jax.nn.sigmoid(pre[:, :n])
        ig = jax.nn.sigmoid(pre[:, n:])
        log_a = c8[d:d + 1, :] * r
        a = jnp.exp(log_a)
        bb = jnp.sqrt(-jnp.tanh(log_a) * (a * a + 1.0)) * (ig * xc)
        return a, bb

    def scan_chunk(a, bb, carry, reverse):
        for s in (1, 2, 4):
            if reverse:
                shift, ok = R - s, rowmod < SUBLANES - s
            else:
                shift, ok = s, rowmod >= s
            a_sh = pltpu.roll(a, shift, 0)
            b_sh = pltpu.roll(bb, shift, 0)
            bb = jnp.where(ok, a * b_sh + bb, bb)
            a = jnp.where(ok, a * a_sh, a)
        out = [None] * groups
        order = range(groups - 1, -1, -1) if reverse else range(groups)
        for gi in order:
            sl = slice(gi * SUBLANES, (gi + 1) * SUBLANES)
            hg = bb[sl, :] + a[sl, :] * carry
            out[gi] = hg
            carry = hg[0:1, :] if reverse else hg[SUBLANES - 1:SUBLANES, :]
        return jnp.concatenate(out, axis=0), carry

    def scan_body(i, carry):
        cf, cb = carry
        rf = pl.multiple_of(i * R, R)
        rb = pl.multiple_of((nc - 1 - i) * R, R)
        a, bb = gates(rf, 0)
        h, cf = scan_chunk(a, bb, cf, False)
        hf[pl.ds(rf, R), :] = h
        a, bb = gates(rb, 1)
        h, cb = scan_chunk(a, bb, cb, True)
        hb[pl.ds(rb, R), :] = h
        return cf, cb

    zero = jnp.zeros((1, n), F32)
    lax.fori_loop(0, nc, scan_body, (zero, zero))

    def out_body(i, carry):
        r = pl.ds(pl.multiple_of(i * R, R), R)
        xg = xg_ref[r, :]
        gelu = 0.5 * xg * (1.0 + jnp.tanh(math.sqrt(2.0 / math.pi) * (xg + 0.044715 * (xg * xg * xg))))
        o_ref[r, :] = ((hf[r, :] + hb[r, :]) * gelu).astype(o_ref.dtype)
        return carry

    lax.fori_loop(0, nc, out_body, 0)


def _lru(u_lru, conv_w, conv_b, a_w, a_b, i_w, i_b, lam, batch, seq, rows=LRU_ROWS):
    T, W2 = u_lru.shape
    W = W2 // 2
    nb, n = a_w.shape[1], a_w.shape[2]
    rows = min(rows, seq)
    w_cat = jnp.concatenate([a_w[0], i_w[0], a_w[1], i_w[1]], axis=-1).astype(BF16)
    b_cat = jnp.stack([a_b[0], i_b[0], a_b[1], i_b[1]], axis=0).reshape(4, nb, n).transpose(1, 0, 2).reshape(nb, 1, 4 * n)
    return pl.pallas_call(
        functools.partial(_lru_kernel, rows=rows),
        grid=(batch, nb),
        in_specs=[
            pl.BlockSpec((seq, n), lambda b, j: (b, j)),
            pl.BlockSpec((seq, n), lambda b, j: (b, nb + j)),
            pl.BlockSpec((conv_w.shape[0], n), lambda b, j: (0, j)),
            pl.BlockSpec((1, n), lambda b, j: (0, j)),
            pl.BlockSpec((None, n, 4 * n), lambda b, j: (j, 0, 0)),
            pl.BlockSpec((None, 1, 4 * n), lambda b, j: (j, 0, 0)),
            pl.BlockSpec((2, n), lambda b, j: (0, j)),
        ],
        out_specs=pl.BlockSpec((seq, n), lambda b, j: (b, j)),
        out_shape=jax.ShapeDtypeStruct((T, W), BF16),
        scratch_shapes=[pltpu.VMEM((seq + 2 * HALO, n), F32), pltpu.VMEM((seq, n), F32), pltpu.VMEM((seq, n), F32)],
        compiler_params=_params(2),
        name="rglru",
    )(u_lru, u_lru, conv_w, conv_b.reshape(1, W), w_cat, b_cat, lam)


def _conv3_kernel(u_ref, w_ref, b_ref, o_ref, xp, *, rows):
    L, n = u_ref.shape
    _fill_padded(u_ref, xp, rows)

    def body(i, carry):
        r0 = pl.multiple_of(i * rows, rows)
        xe = xp[pl.ds(r0, rows + 2 * HALO), :]
        o_ref[pl.ds(r0, rows), :] = _taps(xe, w_ref, b_ref, rows, -1).astype(o_ref.dtype)
        return carry

    lax.fori_loop(0, L // rows, body, 0)


def _conv3(u, w, b, batch, seq, tn=256, rows=256):
    T, N = u.shape
    rows = min(rows, seq)
    blk = pl.BlockSpec((seq, tn), lambda bi, j: (bi, j))
    return pl.pallas_call(
        functools.partial(_conv3_kernel, rows=rows),
        grid=(batch, N // tn),
        in_specs=[blk, pl.BlockSpec((w.shape[0], tn), lambda bi, j: (0, j)), pl.BlockSpec((1, tn), lambda bi, j: (0, j))],
        out_specs=blk,
        out_shape=jax.ShapeDtypeStruct((T, N), BF16),
        scratch_shapes=[pltpu.VMEM((seq + 2 * HALO, tn), F32)],
        compiler_params=_params(2),
        name="conv3",
    )(u, w, b.reshape(1, N))


def _filter_kernel(z_ref, w1_ref, b1_ref, w2_ref, b2_ref, fr_ref, wf_ref, wb_ref, dl_ref, e_ref, o_ref, hid, *, rows):
    L = z_ref.shape[0]
    hi = lax.Precision.HIGHEST

    @pl.when((pl.program_id(0) == 0) & (pl.program_id(1) == 0))
    def _():
        def body(i, carry):
            r = pl.ds(pl.multiple_of(i * rows, rows), rows)
            h1 = jnp.sin(fr_ref[0:1, :] * (jnp.dot(z_ref[r, :], w1_ref[...], precision=hi, preferred_element_type=F32) + b1_ref[...]))
            h2 = jnp.sin(fr_ref[1:2, :] * (jnp.dot(h1, w2_ref[...], precision=hi, preferred_element_type=F32) + b2_ref[...]))
            hid[r, :] = h2
            return carry

        lax.fori_loop(0, L // rows, body, 0)

    def body(i, carry):
        r0 = pl.multiple_of(i * rows, rows)
        r = pl.ds(r0, rows)
        hh = hid[r, :].astype(BF16)
        t = z_ref[r, 0:1]
        win = jnp.exp(-t * jnp.abs(dl_ref[...]))
        hf = _bdot(hh, wf_ref[...].astype(BF16)) * win
        hb = _bdot(hh, wb_ref[...].astype(BF16)) * win
        lag = lax.broadcasted_iota(jnp.int32, hb.shape, 0) + r0
        hb = jnp.where(lag == 0, 0.0, hb)
        e_ref[r, :] = (hf + hb).astype(e_ref.dtype)
        o_ref[r, :] = (hf - hb).astype(o_ref.dtype)
        return carry

    lax.fori_loop(0, L // rows, body, 0)


def _hyena_filters(seq, width, w1, b1, w2, b2, w3, freq, tn=512, rows=512):
    emb, hid_in = w1.shape
    order = w3.shape[1] // (2 * width)
    rows = min(rows, seq)
    tn = min(tn, width)
    nt = width // tn
    t = jnp.linspace(0.0, 1.0, seq, dtype=F32)[:, None]
    bands = (emb - 1) // 2
    f = jnp.linspace(1e-4, bands - 1, bands, dtype=F32)[None, :]
    w = 2.0 * math.pi * jnp.arange(seq, dtype=F32)[:, None] / seq
    z = jnp.concatenate([t, jnp.cos(f * w), -jnp.sin(f * w)], axis=-1)
    hidden = LANES
    pad = lambda a, shape: jnp.zeros(shape, F32).at[tuple(slice(0, d) for d in a.shape)].set(a)
    zp = pad(z, (seq, LANES))
    w1p = pad(w1, (LANES, hidden))
    b1, b2 = pad(b1[None], (1, hidden)), pad(b2[None], (1, hidden))
    w2 = pad(w2, (hidden, hidden))
    w3 = pad(w3, (hidden, w3.shape[1]))
    freq = pad(freq, (2, hidden))
    deltas = jnp.linspace(math.log(HY_TARGET) / HY_SLOW_DECAY, math.log(HY_TARGET) / HY_FAST_DECAY, width, dtype=F32)
    full = lambda shape: pl.BlockSpec(shape, lambda o, j: (0,) * len(shape))
    out = pl.BlockSpec((None, seq, tn), lambda o, j: (o, 0, j))
    return pl.pallas_call(
        functools.partial(_filter_kernel, rows=rows),
        grid=(order, nt),
        in_specs=[
            full((seq, LANES)), full((LANES, hidden)), full((1, hidden)), full((hidden, hidden)), full((1, hidden)),
            full((2, hidden)),
            pl.BlockSpec((hidden, tn), lambda o, j: (0, o * 2 * nt + j)),
            pl.BlockSpec((hidden, tn), lambda o, j: (0, o * 2 * nt + nt + j)),
            pl.BlockSpec((1, tn), lambda o, j: (0, j)),
        ],
        out_specs=[out, out],
        out_shape=[jax.ShapeDtypeStruct((order, seq, width), BF16)] * 2,
        scratch_shapes=[pltpu.VMEM((seq, hidden), F32)],
        compiler_params=_params(2),
        name="hyena_filter",
    )(zp, w1p, b1.reshape(1, hidden), w2, b2.reshape(1, hidden), freq, w3, w3, deltas.reshape(1, width))


def _dft_matrices(seq):
    i = jnp.arange(seq, dtype=jnp.int32)
    period = 4 * seq
    m_kt = ((2 * i[:, None] + 1) * i[None, :]) & (period - 1)
    ang = m_kt.astype(F32) * (2.0 * math.pi / period)
    ang_t = m_kt.T.astype(F32) * (2.0 * math.pi / period)
    return jnp.cos(ang).astype(BF16), jnp.sin(ang).astype(BF16), jnp.cos(ang_t).astype(BF16), jnp.sin(ang_t).astype(BF16)


def _spec_kernel(c_ref, s_ref, e_ref, o_ref, ka_ref, kb_ref):
    ka_ref[...] = _bdot(c_ref[...], e_ref[...])
    kb_ref[...] = _bdot(s_ref[...], o_ref[...])


def _filter_spectrum(cm, sm, e, o, tk=512, tn=512):
    order, L, W = e.shape
    tk, tn = min(tk, L), min(tn, W)
    mat = pl.BlockSpec((tk, L), lambda i, q, j: (i, 0))
    x = pl.BlockSpec((None, L, tn), lambda i, q, j: (q, 0, j))
    out = pl.BlockSpec((None, tk, tn), lambda i, q, j: (q, i, j))
    return pl.pallas_call(
        _spec_kernel,
        grid=(L // tk, order, W // tn),
        in_specs=[mat, mat, x, x],
        out_specs=[out, out],
        out_shape=[jax.ShapeDtypeStruct((order, L, W), F32)] * 2,
        compiler_params=_params(3),
        name="hyena_spectrum",
    )(cm, sm, e, o)


def _dft_fwd_kernel(c_ref, s_ref, x_ref, ka_ref, kb_ref, pa_ref, pb_ref, *, scale):
    x = x_ref[...]
    a = _bdot(c_ref[...], x)
    b = _bdot(s_ref[...], x)
    ka = ka_ref[...] * scale
    kb = kb_ref[...] * scale
    pa_ref[...] = (a * ka - b * kb).astype(pa_ref.dtype)
    pb_ref[...] = (a * kb + b * ka).astype(pb_ref.dtype)


def _dft_fwd(cm, sm, x, x_col, ka, kb, order, tk=512, tn=512):
    B, L, _ = x.shape
    W = ka.shape[2]
    tk, tn = min(tk, L), min(tn, W)
    xo = x_col // tn
    mat = pl.BlockSpec((tk, L), lambda i, b, j: (i, 0))
    kspec = pl.BlockSpec((None, tk, tn), lambda i, b, j: (order, i, j))
    out = pl.BlockSpec((None, tk, tn), lambda i, b, j: (b, i, j))
    return pl.pallas_call(
        functools.partial(_dft_fwd_kernel, scale=1.0 / L),
        grid=(L // tk, B, W // tn),
        in_specs=[mat, mat, pl.BlockSpec((None, L, tn), lambda i, b, j: (b, 0, xo + j)), kspec, kspec],
        out_specs=[out, out],
        out_shape=[jax.ShapeDtypeStruct((B, L, W), BF16)] * 2,
        compiler_params=_params(3),
        name="hyena_dft_fwd",
    )(cm, sm, x, ka, kb)


def _dft_inv_kernel(c_ref, s_ref, pa_ref, pb_ref, x_ref, g_ref, bias_ref, o_ref):
    y = _bdot(c_ref[...], pa_ref[...]) + _bdot(s_ref[...], pb_ref[...])
    x = x_ref[...].astype(F32)
    o_ref[...] = (g_ref[...].astype(F32) * (y + bias_ref[...] * x)).astype(o_ref.dtype)


def _dft_inv(cmt, smt, pa, pb, x, x_col, gate, gate_col, bias, tt=512, tn=512):
    B, L, W = pa.shape
    tt, tn = min(tt, L), min(tn, W)
    xo, go = x_col // tn, gate_col // tn
    mat = pl.BlockSpec((tt, L), lambda i, b, j: (i, 0))
    spec = pl.BlockSpec((None, L, tn), lambda i, b, j: (b, 0, j))
    tile = pl.BlockSpec((None, tt, tn), lambda i, b, j: (b, i, j))
    return pl.pallas_call(
        _dft_inv_kernel,
        grid=(L // tt, B, W // tn),
        in_specs=[mat, mat, spec, spec,
                  pl.BlockSpec((None, tt, tn), lambda i, b, j: (b, i, xo + j)),
                  pl.BlockSpec((None, tt, tn), lambda i, b, j: (b, i, go + j)),
                  pl.BlockSpec((1, tn), lambda i, b, j: (0, j))],
        out_specs=tile,
        out_shape=jax.ShapeDtypeStruct((B, L, W), BF16),
        compiler_params=_params(3),
        name="hyena_dft_inv",
    )(cmt, smt, pa, pb, x, gate, bias.reshape(1, W))


def _hyena_mix(h, dft, w_in, b_in, conv_w, conv_b, f_w1, f_b1, f_w2, f_b2, f_w3, f_freq, bias, batch, seq):
    T = h.shape[0]
    W = w_in.shape[1] // 3
    cm, sm, cmt, smt = dft
    u = _matmul(h, w_in, b_in, out_dtype=BF16)
    uc = _conv3(u, conv_w, conv_b, batch, seq).reshape(batch, seq, 3 * W)
    e, o = _hyena_filters(seq, W, f_w1, f_b1, f_w2, f_b2, f_w3, f_freq)
    ka, kb = _filter_spectrum(cm, sm, e, o)
    z = uc
    for order in range(ka.shape[0]):
        pa, pb = _dft_fwd(cm, sm, z, 0, ka, kb, order)
        z = _dft_inv(cmt, smt, pa, pb, z, 0, uc, (order + 1) * W, bias[order])
    return z.reshape(T, W)


def kernel(x, c, cond_w, cond_b, mod_w, mod_b, norm_g, ffn_w1, ffn_w3, ffn_w2, ev_in_w, ev_out_w, ret_gn_g, lru_conv_w, lru_conv_b, lru_a_w, lru_a_b, lru_i_w, lru_i_b, lru_lambda, hy_in_w, hy_in_b, hy_conv_w, hy_conv_b, hy_f_w1, hy_f_b1, hy_f_w2, hy_f_b2, hy_f_w3, hy_f_freq, hy_bias, hy_out_w, final_g):
    B, L, D = x.shape
    depth = mod_w.shape[0]
    T = B * L
    mods = _conditioning(c, cond_w, cond_b, mod_w, mod_b)
    dft = _dft_matrices(L) if depth > 1 else None
    xt = x.reshape(T, D)

    def ffn(xt, layer, which, mod, idx):
        shift, scale, gate = (mod[:, j:j + 1, :] for j in idx)
        h = _norm_mod(xt, norm_g[layer, 2 * which], scale, shift, L)
        u = _ffn_up(h, ffn_w1[layer, which], ffn_w3[layer, which])
        return _res_mm([u], ffn_w2[layer, which], xt, gate, L, 0.5)

    for layer in range(depth):
        mod = mods[layer]
        xt = ffn(xt, layer, 0, mod, (0, 1, 2))
        h = _norm_mod(xt, norm_g[layer, 1], mod[:, 4:5, :], mod[:, 3:4, :], L)
        gate = mod[:, 5:6, :]
        if layer % 2 == 0:
            e = layer // 2
            RW = ret_gn_g.shape[1]
            u_ret = _matmul(h, ev_in_w[e], w_col=0, n_out=4 * RW, out_dtype=BF16)
            u_lru = _matmul(h, ev_in_w[e], w_col=4 * RW, out_dtype=F32)
            ret = _retention(u_ret, ret_gn_g[e], B, L, RET_HEADS)
            lru = _lru(u_lru, lru_conv_w[e], lru_conv_b[e], lru_a_w[e], lru_a_b[e], lru_i_w[e], lru_i_b[e],
                       lru_lambda[e], B, L)
            xt = _res_mm([ret, lru], ev_out_w[e], xt, gate, L, 1.0)
        else:
            o = layer // 2
            z = _hyena_mix(h, dft, hy_in_w[o], hy_in_b[o], hy_conv_w[o], hy_conv_b[o], hy_f_w1[o], hy_f_b1[o],
                           hy_f_w2[o], hy_f_b2[o], hy_f_w3[o], hy_f_freq[o], hy_bias[o], B, L)
            xt = _res_mm([z], hy_out_w[o], xt, gate, L, 1.0)
        xt = ffn(xt, layer, 1, mod, (6, 7, 8))
    return _final_norm(xt, final_g).reshape(B, L, D)
```

```python
import functools
import math

import jax
import jax.numpy as jnp
from jax import lax
from jax.experimental import pallas as pl
from jax.experimental.pallas import tpu as pltpu

F32 = jnp.float32
BF16 = jnp.bfloat16

EPS = 1e-6
RET_HEADS = 8
RET_CHUNK = 256
ROPE_BASE = 10000.0
LRU_C = 8.0
LRU_ROWS = 128
HY_FAST_DECAY = 0.3
HY_SLOW_DECAY = 1.5
HY_TARGET = 1e-2
FFT_R2 = 128
SUBLANES = 8
LANES = 128
HALO = SUBLANES
VMEM_LIMIT_BYTES = 62 * 1024 * 1024


def _params(n_axes, vmem=VMEM_LIMIT_BYTES):
    return pltpu.CompilerParams(dimension_semantics=("arbitrary",) * n_axes, vmem_limit_bytes=vmem)


def _bdot(a, b):
    return jnp.dot(a, b, preferred_element_type=F32)


def _norm_mod_kernel(x_ref, g_ref, sc_ref, sh_ref, o_ref):
    x = x_ref[...]
    y = x * lax.rsqrt(jnp.mean(x * x, axis=-1, keepdims=True) + EPS) * g_ref[...]
    o_ref[...] = (y * (1.0 + sc_ref[...]) + sh_ref[...]).astype(o_ref.dtype)


def _norm_kernel(x_ref, g_ref, o_ref):
    x = x_ref[...]
    y = x * lax.rsqrt(jnp.mean(x * x, axis=-1, keepdims=True) + EPS) * g_ref[...]
    o_ref[...] = y.astype(o_ref.dtype)


def _norm_mod(x, g, scale, shift, seq, tm=256):
    T, D = x.shape
    row = pl.BlockSpec((tm, D), lambda i: (i, 0))
    mod = pl.BlockSpec((None, 1, D), lambda i: ((i * tm) // seq, 0, 0))
    return pl.pallas_call(
        _norm_mod_kernel,
        grid=(T // tm,),
        in_specs=[row, pl.BlockSpec((1, D), lambda i: (0, 0)), mod, mod],
        out_specs=row,
        out_shape=jax.ShapeDtypeStruct((T, D), BF16),
        compiler_params=_params(1),
        name="norm_mod",
    )(x, g.reshape(1, D), scale, shift)


def _final_norm(x, g, tm=256):
    T, D = x.shape
    row = pl.BlockSpec((tm, D), lambda i: (i, 0))
    return pl.pallas_call(
        _norm_kernel,
        grid=(T // tm,),
        in_specs=[row, pl.BlockSpec((1, D), lambda i: (0, 0))],
        out_specs=row,
        out_shape=jax.ShapeDtypeStruct((T, D), x.dtype),
        compiler_params=_params(1),
        name="final_norm",
    )(x, g.reshape(1, D))


def _resident(tm, K):
    return pl.BlockSpec((tm, K), lambda m, n: (m, 0))


def _wspec(w_idx, K, tn, row_blk=0, col_blk=0):
    lead = tuple(w_idx)
    return pl.BlockSpec((None,) * len(lead) + (K, tn), lambda m, n: lead + (row_blk, col_blk + n))


def _mm_kernel(*refs, has_bias):
    if has_bias:
        a_ref, w_ref, b_ref, o_ref = refs
    else:
        a_ref, w_ref, o_ref = refs
    acc = _bdot(a_ref[...], w_ref[...].astype(BF16))
    if has_bias:
        acc = acc + b_ref[...]
    o_ref[...] = acc.astype(o_ref.dtype)


def _matmul(a, w, w_idx, bias=None, *, w_col=0, n_out=None, out_dtype, tm=2048, tn=512):
    M, K = a.shape
    N = w.shape[-1] - w_col if n_out is None else n_out
    tm, tn = min(tm, M), min(tn, N)
    in_specs = [_resident(tm, K), _wspec(w_idx, K, tn, col_blk=w_col // tn)]
    args = [a, w]
    if bias is not None:
        in_specs.append(pl.BlockSpec((1, tn), lambda m, n: (0, n)))
        args.append(bias.reshape(1, N))
    return pl.pallas_call(
        functools.partial(_mm_kernel, has_bias=bias is not None),
        grid=(M // tm, N // tn),
        in_specs=in_specs,
        out_specs=pl.BlockSpec((tm, tn), lambda m, n: (m, n)),
        out_shape=jax.ShapeDtypeStruct((M, N), out_dtype),
        compiler_params=_params(2),
        name="matmul",
    )(*args)


def _ffn_up_kernel(h_ref, w1_ref, w3_ref, o_ref):
    h = h_ref[...]
    a = _bdot(h, w1_ref[...].astype(BF16))
    b = _bdot(h, w3_ref[...].astype(BF16))
    o_ref[...] = (a * jax.nn.sigmoid(a) * b).astype(o_ref.dtype)


def _ffn_up(h, w1, w3, w_idx, tm=2048, tn=256):
    M, K = h.shape
    N = w1.shape[-1]
    tm = min(tm, M)
    wspec = _wspec(w_idx, K, tn)
    return pl.pallas_call(
        _ffn_up_kernel,
        grid=(M // tm, N // tn),
        in_specs=[_resident(tm, K), wspec, wspec],
        out_specs=pl.BlockSpec((tm, tn), lambda m, n: (m, n)),
        out_shape=jax.ShapeDtypeStruct((M, N), BF16),
        compiler_params=_params(2),
        name="ffn_up",
    )(h, w1, w3)


def _res_mm_kernel(*refs, n_pairs, coef):
    x_ref, g_ref, o_ref = refs[2 * n_pairs:]
    acc = _bdot(refs[0][...], refs[1][...].astype(BF16))
    for i in range(1, n_pairs):
        acc = acc + _bdot(refs[2 * i][...], refs[2 * i + 1][...].astype(BF16))
    o_ref[...] = x_ref[...] + (coef * (1.0 + g_ref[...])) * acc


def _res_mm(acts, w, w_idx, x, gate, seq, coef, tm=1024, tn=256):
    T, N = x.shape
    tm = min(tm, T, seq)
    in_specs = []
    for i, a in enumerate(acts):
        K = a.shape[1]
        in_specs += [_resident(tm, K), _wspec(w_idx, K, tn, row_blk=i)]
    tile = pl.BlockSpec((tm, tn), lambda m, n: (m, n))
    in_specs += [tile, pl.BlockSpec((None, 1, tn), lambda m, n: ((m * tm) // seq, 0, n))]
    args = [v for a in acts for v in (a, w)]
    return pl.pallas_call(
        functools.partial(_res_mm_kernel, n_pairs=len(acts), coef=coef),
        grid=(T // tm, N // tn),
        in_specs=in_specs,
        out_specs=tile,
        out_shape=jax.ShapeDtypeStruct((T, N), x.dtype),
        compiler_params=_params(2),
        name="res_mm",
    )(*args, x, gate)


def _cond_kernel(c_ref, w_ref, b_ref, o_ref):
    a = _bdot(c_ref[...].astype(BF16), w_ref[...].astype(BF16)) + b_ref[...]
    o_ref[...] = a * jax.nn.sigmoid(a)


def _mod_kernel(h_ref, w_ref, b_ref, o_ref):
    o_ref[...] = _bdot(h_ref[...].astype(BF16), w_ref[...].astype(BF16)) + b_ref[...]


def _conditioning(c, cond_w, cond_b, mod_w, mod_b, tn=2048):
    B, D = c.shape
    tn = min(tn, D)
    H = cond_w.shape[1]
    depth, _, NM = mod_w.shape
    rows = SUBLANES
    c_pad = jnp.zeros((rows, D), c.dtype).at[:B].set(c)
    c_hid = pl.pallas_call(
        _cond_kernel,
        out_shape=jax.ShapeDtypeStruct((rows, H), F32),
        compiler_params=_params(0),
        name="cond",
    )(c_pad, cond_w, cond_b.reshape(1, H))
    mod = pl.pallas_call(
        _mod_kernel,
        grid=(depth, NM // tn),
        in_specs=[
            pl.BlockSpec((rows, H), lambda l, n: (0, 0)),
            pl.BlockSpec((None, H, tn), lambda l, n: (l, 0, n)),
            pl.BlockSpec((None, 1, tn), lambda l, n: (l, 0, n)),
        ],
        out_specs=pl.BlockSpec((None, rows, tn), lambda l, n: (l, 0, n)),
        out_shape=jax.ShapeDtypeStruct((depth, rows, NM), F32),
        compiler_params=_params(2),
        name="mod",
    )(c_hid, mod_w, mod_b.reshape(depth, 1, NM))
    return mod[:, :B].reshape(depth, B, NM // D, D)


def _retention_kernel(lg_ref, q_ref, k_ref, v_ref, g_ref, cos_ref, sin_ref, gn_ref, o_ref,
                      qr, kr, yacc, st, mask_s, xif_s, xib_s, zf_s, zb_s, *, chunk, k_scale):
    L, Dh = q_ref.shape
    half = Dh // 2
    C = chunk
    nc = L // C
    lg = lg_ref[pl.program_id(1)]
    contract_last = (((1,), (1,)), ((), ()))
    contract_rows = (((0,), (0,)), ((), ()))

    def rot_body(i, carry):
        r = pl.ds(pl.multiple_of(i * C, C), C)
        cs = cos_ref[r, :]
        sn = sin_ref[r, :]
        for src, dst, sc in ((q_ref, qr, 1.0), (k_ref, kr, k_scale)):
            x = src[r, :].astype(F32)
            x1 = x[:, :half]
            x2 = x[:, half:]
            y = jnp.concatenate([x1 * cs - x2 * sn, x1 * sn + x2 * cs], axis=-1) * sc
            dst[r, :] = y.astype(BF16)
        return carry

    lax.fori_loop(0, nc, rot_body, 0)

    n_i = lax.broadcasted_iota(jnp.int32, (C, C), 0)
    m_i = lax.broadcasted_iota(jnp.int32, (C, C), 1)
    mask_s[...] = jnp.exp(jnp.abs(n_i - m_i).astype(F32) * lg)
    row = lax.broadcasted_iota(jnp.int32, (C, Dh), 0).astype(F32)
    xif_s[...] = jnp.exp((row + 1.0) * lg)
    xib_s[...] = jnp.exp((C - row) * lg)
    zf_s[...] = jnp.exp((C - 1.0 - row) * lg)
    zb_s[...] = jnp.exp(row * lg)
    decay = jnp.exp(jnp.full((1, Dh), float(C), F32) * lg)

    def chunk_refs(c):
        r = pl.ds(pl.multiple_of(c * C, C), C)
        return r, qr[r, :], kr[r, :], v_ref[r, :]

    def advance(kc, vc, zeta):
        kz = (kc.astype(F32) * zeta).astype(BF16)
        st[...] = st[...] * decay + lax.dot_general(kz, vc, contract_rows, preferred_element_type=F32)

    st[...] = jnp.zeros_like(st)

    def fwd_body(c, carry):
        r, qc, kc, vc = chunk_refs(c)
        s = lax.dot_general(qc, kc, contract_last, preferred_element_type=F32) * mask_s[...]
        y = _bdot(s.astype(BF16), vc)
        y = y + _bdot(qc, st[...].astype(BF16)) * xif_s[...]
        yacc[r, :] = y
        advance(kc, vc, zf_s[...])
        return carry

    lax.fori_loop(0, nc, fwd_body, 0)

    st[...] = jnp.zeros_like(st)

    def bwd_body(i, carry):
        r, qc, kc, vc = chunk_refs(nc - 1 - i)
        y = yacc[r, :] + _bdot(qc, st[...].astype(BF16)) * xib_s[...]
        mu = jnp.mean(y, axis=-1, keepdims=True)
        d = y - mu
        var = jnp.mean(d * d, axis=-1, keepdims=True)
        yn = d * lax.rsqrt(var + EPS) * gn_ref[...]
        gg = g_ref[r, :].astype(F32)
        o_ref[r, :] = (gg * jax.nn.sigmoid(gg) * yn).astype(o_ref.dtype)
        advance(kc, vc, zb_s[...])
        return carry

    lax.fori_loop(0, nc, bwd_body, 0)


def _retention(u_ret, gn_g, batch, seq, n_heads, chunk=RET_CHUNK):
    T, W4 = u_ret.shape
    RW = W4 // 4
    Dh = RW // n_heads
    half = Dh // 2
    H = n_heads
    chunk = min(chunk, seq)
    log_g = jnp.log1p(-(2.0 ** (-5.0 - jnp.arange(H, dtype=F32))))
    inv = ROPE_BASE ** (-jnp.arange(half, dtype=F32) / half)
    ang = jnp.arange(seq, dtype=F32)[:, None] * inv[None, :]
    cos, sin = jnp.cos(ang), jnp.sin(ang)

    def col(off):
        return pl.BlockSpec((seq, Dh), lambda b, h: (b, off + h))

    table = pl.BlockSpec((seq, half), lambda b, h: (0, 0))
    return pl.pallas_call(
        functools.partial(_retention_kernel, chunk=chunk, k_scale=Dh ** -0.5),
        grid=(batch, H),
        in_specs=[pl.BlockSpec(memory_space=pltpu.SMEM), col(0), col(H), col(2 * H), col(3 * H), table, table,
                  pl.BlockSpec((1, Dh), lambda b, h: (0, h))],
        out_specs=pl.BlockSpec((seq, Dh), lambda b, h: (b, h)),
        out_shape=jax.ShapeDtypeStruct((T, RW), BF16),
        scratch_shapes=[
            pltpu.VMEM((seq, Dh), BF16), pltpu.VMEM((seq, Dh), BF16), pltpu.VMEM((seq, Dh), F32),
            pltpu.VMEM((Dh, Dh), F32), pltpu.VMEM((chunk, chunk), F32),
            pltpu.VMEM((chunk, Dh), F32), pltpu.VMEM((chunk, Dh), F32),
            pltpu.VMEM((chunk, Dh), F32), pltpu.VMEM((chunk, Dh), F32),
        ],
        compiler_params=_params(2),
        name="retention",
    )(log_g, u_ret, u_ret, u_ret, u_ret, cos, sin, gn_g.reshape(1, RW))


def _fill_padded(src_ref, xp, rows):
    L, n = src_ref.shape
    zeros = jnp.zeros((HALO, n), xp.dtype)
    xp[0:HALO, :] = zeros
    xp[HALO + L:HALO + L + HALO, :] = zeros

    def body(i, carry):
        r0 = pl.multiple_of(i * rows, rows)
        xp[pl.ds(r0 + HALO, rows), :] = src_ref[pl.ds(r0, rows), :].astype(xp.dtype)
        return carry

    lax.fori_loop(0, L // rows, body, 0)


def _taps(xe, w_ref, b_ref, rows, first):
    acc = b_ref[...]
    for k in range(w_ref.shape[0]):
        o = HALO + first + k
        acc = acc + w_ref[k:k + 1, :] * xe[o:o + rows, :]
    return acc


def _lru_kernel(xr_ref, xg_ref, cw_ref, cb_ref, w_ref, b_ref, lam_ref, o_ref, xp, hf, hb, *, rows):
    L, n = xr_ref.shape
    R = rows
    nc = L // R
    groups = R // SUBLANES
    _fill_padded(xr_ref, xp, R)
    nl = -lam_ref[...]
    c8 = -LRU_C * (jnp.maximum(nl, 0.0) + jnp.log1p(jnp.exp(-jnp.abs(nl))))
    rowmod = lax.broadcasted_iota(jnp.int32, (R, n), 0) & (SUBLANES - 1)

    def gates(r0, d):
        xe = xp[pl.ds(r0, R + 2 * HALO), :]
        xc = _taps(xe, cw_ref, cb_ref, R, -2)
        pre = _bdot(xc.astype(BF16), w_ref[:, 2 * n * d:2 * n * (d + 1)]) + b_ref[:, 2 * n * d:2 * n * (d + 1)]
        r = jax.nn.sigmoid(pre[:, :n])
        ig = jax.nn.sigmoid(pre[:, n:])
        log_a = c8[d:d + 1, :] * r
        a = jnp.exp(log_a)
        bb = jnp.sqrt(-jnp.tanh(log_a) * (a * a + 1.0)) * (ig * xc)
        return a, bb

    def scan_chunk(a, bb, carry, reverse):
        for s in (1, 2, 4):
            if reverse:
                shift, ok = R - s, rowmod < SUBLANES - s
            else:
                shift, ok = s, rowmod >= s
            a_sh = pltpu.roll(a, shift, 0)
            b_sh = pltpu.roll(bb, shift, 0)
            bb = jnp.where(ok, a * b_sh + bb, bb)
            a = jnp.where(ok, a * a_sh, a)
        out = [None] * groups
        order = range(groups - 1, -1, -1) if reverse else range(groups)
        for gi in order:
            sl = slice(gi * SUBLANES, (gi + 1) * SUBLANES)
            hg = bb[sl, :] + a[sl, :] * carry
            out[gi] = hg
            carry = hg[0:1, :] if reverse else hg[SUBLANES - 1:SUBLANES, :]
        return jnp.concatenate(out, axis=0), carry

    def scan_body(i, carry):
        cf, cb = carry
        rf = pl.multiple_of(i * R, R)
        rb = pl.multiple_of((nc - 1 - i) * R, R)
        a, bb = gates(rf, 0)
        h, cf = scan_chunk(a, bb, cf, False)
        hf[pl.ds(rf, R), :] = h
        a, bb = gates(rb, 1)
        h, cb = scan_chunk(a, bb, cb, True)
        hb[pl.ds(rb, R), :] = h
        return cf, cb

    zero = jnp.zeros((1, n), F32)
    lax.fori_loop(0, nc, scan_body, (zero, zero))

    def out_body(i, carry):
        r = pl.ds(pl.multiple_of(i * R, R), R)
        xg = xg_ref[r, :]
        gelu = 0.5 * xg * (1.0 + jnp.tanh(math.sqrt(2.0 / math.pi) * (xg + 0.044715 * (xg * xg * xg))))
        o_ref[r, :] = ((hf[r, :] + hb[r, :]) * gelu).astype(o_ref.dtype)
        return carry

    lax.fori_loop(0, nc, out_body, 0)


def _lru(u_lru, conv_w, conv_b, a_w, a_b, i_w, i_b, lam, batch, seq, rows=LRU_ROWS):
    T, W2 = u_lru.shape
    W = W2 // 2
    nb, n = a_w.shape[1], a_w.shape[2]
    rows = min(rows, seq)
    w_cat = jnp.concatenate([a_w[0], i_w[0], a_w[1], i_w[1]], axis=-1).astype(BF16)
    b_cat = jnp.stack([a_b[0], i_b[0], a_b[1], i_b[1]], axis=0).reshape(4, nb, n).transpose(1, 0, 2).reshape(nb, 1, 4 * n)
    return pl.pallas_call(
        functools.partial(_lru_kernel, rows=rows),
        grid=(batch, nb),
        in_specs=[
            pl.BlockSpec((seq, n), lambda b, j: (b, j)),
            pl.BlockSpec((seq, n), lambda b, j: (b, nb + j)),
            pl.BlockSpec((conv_w.shape[0], n), lambda b, j: (0, j)),
            pl.BlockSpec((1, n), lambda b, j: (0, j)),
            pl.BlockSpec((None, n, 4 * n), lambda b, j: (j, 0, 0)),
            pl.BlockSpec((None, 1, 4 * n), lambda b, j: (j, 0, 0)),
            pl.BlockSpec((2, n), lambda b, j: (0, j)),
        ],
        out_specs=pl.BlockSpec((seq, n), lambda b, j: (b, j)),
        out_shape=jax.ShapeDtypeStruct((T, W), BF16),
        scratch_shapes=[pltpu.VMEM((seq + 2 * HALO, n), F32), pltpu.VMEM((seq, n), F32), pltpu.VMEM((seq, n), F32)],
        compiler_params=_params(2),
        name="rglru",
    )(u_lru, u_lru, conv_w, conv_b.reshape(1, W), w_cat, b_cat, lam)


def _conv3_kernel(u_ref, w_ref, b_ref, o_ref, xp, *, rows):
    L, n = u_ref.shape
    _fill_padded(u_ref, xp, rows)

    def body(i, carry):
        r0 = pl.multiple_of(i * rows, rows)
        xe = xp[pl.ds(r0, rows + 2 * HALO), :]
        o_ref[pl.ds(r0, rows), :] = _taps(xe, w_ref, b_ref, rows, -1).astype(o_ref.dtype)
        return carry

    lax.fori_loop(0, L // rows, body, 0)


def _conv3(u, w, b, batch, seq, parts=3, tn=256, rows=256):
    T, N = u.shape
    W = N // parts
    rows = min(rows, seq)
    nw = W // tn
    return pl.pallas_call(
        functools.partial(_conv3_kernel, rows=rows),
        grid=(batch, parts, nw),
        in_specs=[pl.BlockSpec((seq, tn), lambda bi, p, j: (bi, p * nw + j)),
                  pl.BlockSpec((w.shape[0], tn), lambda bi, p, j: (0, p * nw + j)),
                  pl.BlockSpec((1, tn), lambda bi, p, j: (0, p * nw + j))],
        out_specs=pl.BlockSpec((None, seq, tn), lambda bi, p, j: (p, bi, j)),
        out_shape=jax.ShapeDtypeStruct((parts, T, W), BF16),
        scratch_shapes=[pltpu.VMEM((seq + 2 * HALO, tn), F32)],
        compiler_params=_params(3),
        name="conv3",
    )(u, w, b.reshape(1, N))


def _filter_kernel(z_ref, w1_ref, b1_ref, w2_ref, b2_ref, fr_ref, wf_ref, wb_ref, dl_ref, e_ref, o_ref, hid, *, rows):
    L = z_ref.shape[0]
    hi = lax.Precision.HIGHEST

    @pl.when((pl.program_id(0) == 0) & (pl.program_id(1) == 0))
    def _():
        def body(i, carry):
            r = pl.ds(pl.multiple_of(i * rows, rows), rows)
            h1 = jnp.sin(fr_ref[0:1, :] * (jnp.dot(z_ref[r, :], w1_ref[...], precision=hi, preferred_element_type=F32) + b1_ref[...]))
            h2 = jnp.sin(fr_ref[1:2, :] * (jnp.dot(h1, w2_ref[...], precision=hi, preferred_element_type=F32) + b2_ref[...]))
            hid[r, :] = h2
            return carry

        lax.fori_loop(0, L // rows, body, 0)

    def body(i, carry):
        r0 = pl.multiple_of(i * rows, rows)
        r = pl.ds(r0, rows)
        hh = hid[r, :].astype(BF16)
        t = z_ref[r, 0:1]
        win = jnp.exp(-t * jnp.abs(dl_ref[...]))
        hf = _bdot(hh, wf_ref[...].astype(BF16)) * win
        hb = _bdot(hh, wb_ref[...].astype(BF16)) * win
        lag = lax.broadcasted_iota(jnp.int32, hb.shape, 0) + r0
        hb = jnp.where(lag == 0, 0.0, hb)
        e_ref[r, :] = (hf + hb).astype(e_ref.dtype)
        o_ref[r, :] = (hf - hb).astype(o_ref.dtype)
        return carry

    lax.fori_loop(0, L // rows, body, 0)


def _hyena_filters(seq, width, w1, b1, w2, b2, w3, freq, tn=512, rows=512):
    emb, hid_in = w1.shape
    order = w3.shape[1] // (2 * width)
    rows = min(rows, seq)
    tn = min(tn, width)
    nt = width // tn
    t = jnp.linspace(0.0, 1.0, seq, dtype=F32)[:, None]
    bands = (emb - 1) // 2
    f = jnp.linspace(1e-4, bands - 1, bands, dtype=F32)[None, :]
    w = 2.0 * math.pi * jnp.arange(seq, dtype=F32)[:, None] / seq
    z = jnp.concatenate([t, jnp.cos(f * w), -jnp.sin(f * w)], axis=-1)
    hidden = LANES
    pad = lambda a, shape: jnp.zeros(shape, F32).at[tuple(slice(0, d) for d in a.shape)].set(a)
    zp = pad(z, (seq, LANES))
    w1p = pad(w1, (LANES, hidden))
    b1, b2 = pad(b1[None], (1, hidden)), pad(b2[None], (1, hidden))
    w2 = pad(w2, (hidden, hidden))
    w3 = pad(w3, (hidden, w3.shape[1]))
    freq = pad(freq, (2, hidden))
    deltas = jnp.linspace(math.log(HY_TARGET) / HY_SLOW_DECAY, math.log(HY_TARGET) / HY_FAST_DECAY, width, dtype=F32)
    full = lambda shape: pl.BlockSpec(shape, lambda o, j: (0,) * len(shape))
    out = pl.BlockSpec((None, seq, tn), lambda o, j: (o, 0, j))
    return pl.pallas_call(
        functools.partial(_filter_kernel, rows=rows),
        grid=(order, nt),
        in_specs=[
            full((seq, LANES)), full((LANES, hidden)), full((1, hidden)), full((hidden, hidden)), full((1, hidden)),
            full((2, hidden)),
            pl.BlockSpec((hidden, tn), lambda o, j: (0, o * 2 * nt + j)),
            pl.BlockSpec((hidden, tn), lambda o, j: (0, o * 2 * nt + nt + j)),
            pl.BlockSpec((1, tn), lambda o, j: (0, j)),
        ],
        out_specs=[out, out],
        out_shape=[jax.ShapeDtypeStruct((order, seq, width), BF16)] * 2,
        scratch_shapes=[pltpu.VMEM((seq, hidden), F32)],
        compiler_params=_params(2),
        name="hyena_filter",
    )(zp, w1p, b1.reshape(1, hidden), w2, b2.reshape(1, hidden), freq, w3, w3, deltas.reshape(1, width))


def _fft_tables(seq, r2=FFT_R2):
    r2 = min(r2, seq // 16)
    r1 = seq // r2
    a1 = jnp.arange(r1, dtype=jnp.int32)
    a2 = jnp.arange(r2, dtype=jnp.int32)
    ang1 = (((2 * a1[:, None] + 1) * a1[None, :]) % (4 * r1)).astype(F32) * (2.0 * math.pi / (4 * r1))
    c1, s1 = jnp.cos(ang1), jnp.sin(ang1)
    fwd1 = jnp.concatenate([c1, -s1], axis=0).astype(BF16)
    inv1 = jnp.concatenate([c1.T, -s1.T], axis=1).astype(BF16)
    k = a1[:, None, None] + 2 * r1 * a2[None, :, None]
    ang2 = (((2 * k + 1) * a2[None, None, :]) % (4 * seq)).astype(F32) * (2.0 * math.pi / (4 * seq))
    c2, s2 = jnp.cos(ang2), jnp.sin(ang2)
    fwd2 = jnp.concatenate([jnp.concatenate([c2, s2], axis=2), jnp.concatenate([-s2, c2], axis=2)], axis=1)
    fwd2 = fwd2.astype(BF16)
    return dict(r1=r1, r2=r2, fwd1=fwd1, inv1=inv1, fwd2=fwd2, inv2=jnp.swapaxes(fwd2, 1, 2))


def _time_spec(lead, r1, n, tn):
    lead = tuple(lead)
    return pl.BlockSpec((None,) * (len(lead) + 1) + (r1, n, tn), lambda b, s, j: lead + (b, 0, s, j))


def _fft_s1_kernel(m_ref, x_ref, o_ref):
    n = x_ref.shape[1]
    xt = pltpu.einshape("tjc->jtc", x_ref[...])
    y = jnp.stack([_bdot(m_ref[...], xt[j]) for j in range(n)], axis=0).astype(o_ref.dtype)
    o_ref[...] = pltpu.einshape("jkc->kjc", y)


def _fft_stage1(tab, x, lead, width, n=16, tn=1024):
    r1, r2 = tab["r1"], tab["r2"]
    B = x.shape[len(lead)]
    n, tn = min(n, r2), min(tn, width)
    return pl.pallas_call(
        _fft_s1_kernel,
        grid=(B, r2 // n, width // tn),
        in_specs=[pl.BlockSpec((2 * r1, r1), lambda b, s, j: (0, 0)), _time_spec(lead, r1, n, tn)],
        out_specs=_time_spec((), 2 * r1, n, tn),
        out_shape=jax.ShapeDtypeStruct((B, 2 * r1, r2, width), BF16),
        compiler_params=_params(3),
        name="hyena_fft_stage1",
    )(tab["fwd1"], x)


def _k1_specs(r2, n, tn):
    return [pl.BlockSpec((None, 2, None, r2, tn), lambda b, s, j, i=i: (b, 0, s * n + i, 0, j)) for i in range(n)]


def _fft_spec_kernel(a_ref, *refs, n):
    ye, yo, o_ref = refs[:n], refs[n:2 * n], refs[2 * n]
    r2 = a_ref.shape[1] // 2
    for i in range(n):
        xe = _bdot(a_ref[i], ye[i][...].reshape(2 * r2, -1))
        xo = _bdot(a_ref[i], yo[i][...].reshape(2 * r2, -1))
        o_ref[i, :r2, :] = xe[:r2]
        o_ref[i, r2:, :] = xo[r2:]


def _fft_filter_spectrum(tab, ye, yo, n=8, tn=1024):
    r1, r2 = tab["r1"], tab["r2"]
    orders, W = ye.shape[0], ye.shape[-1]
    n, tn = min(n, r1), min(tn, W)
    return pl.pallas_call(
        functools.partial(_fft_spec_kernel, n=n),
        grid=(orders, r1 // n, W // tn),
        in_specs=[pl.BlockSpec((n, 2 * r2, 2 * r2), lambda b, s, j: (s, 0, 0))] + _k1_specs(r2, n, tn) * 2,
        out_specs=pl.BlockSpec((None, n, 2 * r2, tn), lambda b, s, j: (b, s, 0, j)),
        out_shape=jax.ShapeDtypeStruct((orders, r1, 2 * r2, W), F32),
        compiler_params=_params(3),
        name="hyena_fft_spectrum",
    )(tab["fwd2"], *([ye] * n), *([yo] * n))


def _fft_mid_kernel(a_ref, at_ref, k_ref, *refs, n, scale):
    ys, o_ref = refs[:n], refs[n]
    r2 = a_ref.shape[1] // 2
    for i in range(n):
        x = _bdot(a_ref[i], ys[i][...].reshape(2 * r2, -1))
        xr, xi = x[:r2], x[r2:]
        kr = k_ref[i, :r2, :] * scale
        ki = k_ref[i, r2:, :] * scale
        p = jnp.concatenate([xr * kr - xi * ki, xr * ki + xi * kr], axis=0).astype(BF16)
        q = _bdot(at_ref[i], p).astype(o_ref.dtype)
        o_ref[0, i] = q[:r2]
        o_ref[1, i] = q[r2:]


def _fft_mid(tab, y, kspec, order, n=8, tn=1024):
    r1, r2 = tab["r1"], tab["r2"]
    B, W = y.shape[0], y.shape[-1]
    n, tn = min(n, r1), min(tn, W)
    mat = pl.BlockSpec((n, 2 * r2, 2 * r2), lambda b, s, j: (s, 0, 0))
    return pl.pallas_call(
        functools.partial(_fft_mid_kernel, n=n, scale=1.0 / (r1 * r2)),
        grid=(B, r1 // n, W // tn),
        in_specs=[mat, mat, pl.BlockSpec((None, n, 2 * r2, tn), lambda b, s, j: (order, s, 0, j))]
        + _k1_specs(r2, n, tn),
        out_specs=pl.BlockSpec((None, 2, n, r2, tn), lambda b, s, j: (b, 0, s, 0, j)),
        out_shape=jax.ShapeDtypeStruct((B, 2, r1, r2, W), BF16),
        compiler_params=_params(3),
        name="hyena_fft_mid",
    )(tab["fwd2"], tab["inv2"], kspec, *([y] * n))


def _fft_out_kernel(m_ref, bias_ref, q_ref, x_ref, g_ref, o_ref):
    n = q_ref.shape[1]
    qt = pltpu.einshape("kjc->jkc", q_ref[...])
    y = jnp.stack([_bdot(m_ref[...], qt[j]) for j in range(n)], axis=0)
    y = pltpu.einshape("jtc->tjc", y)
    x = x_ref[...].astype(F32)
    o_ref[...] = (g_ref[...].astype(F32) * (y + bias_ref[...] * x)).astype(o_ref.dtype)


def _fft_out(tab, q, x, x_lead, gate, gate_lead, bias, n=16, tn=1024):
    r1, r2 = tab["r1"], tab["r2"]
    B = q.shape[0]
    W = bias.shape[0]
    n, tn = min(n, r2), min(tn, W)
    return pl.pallas_call(
        _fft_out_kernel,
        grid=(B, r2 // n, W // tn),
        in_specs=[pl.BlockSpec((r1, 2 * r1), lambda b, s, j: (0, 0)), pl.BlockSpec((1, tn), lambda b, s, j: (0, j)),
                  _time_spec((), 2 * r1, n, tn), _time_spec(x_lead, r1, n, tn), _time_spec(gate_lead, r1, n, tn)],
        out_specs=_time_spec((), r1, n, tn),
        out_shape=jax.ShapeDtypeStruct((B, r1, r2, W), BF16),
        compiler_params=_params(3),
        name="hyena_fft_out",
    )(tab["inv1"], bias.reshape(1, W), q, x, gate)


def _hyena_mix(h, tab, w_in, w_idx, b_in, conv_w, conv_b, f_w1, f_b1, f_w2, f_b2, f_w3, f_freq, bias, batch, seq):
    T = h.shape[0]
    W = w_in.shape[-1] // 3
    r1, r2 = tab["r1"], tab["r2"]
    u = _matmul(h, w_in, w_idx, b_in, out_dtype=BF16)
    uc = _conv3(u, conv_w, conv_b, batch, seq).reshape(3, batch, r1, r2, W)
    e, o = _hyena_filters(seq, W, f_w1, f_b1, f_w2, f_b2, f_w3, f_freq)
    orders = e.shape[0]
    split = lambda y: y.reshape(y.shape[0], 2, r1, r2, W)
    ye = _fft_stage1(tab, e.reshape(orders, r1, r2, W), (), W)
    yo = _fft_stage1(tab, o.reshape(orders, r1, r2, W), (), W)
    kspec = _fft_filter_spectrum(tab, split(ye), split(yo))
    z, z_lead = uc, (0,)
    for order in range(orders):
        y = _fft_stage1(tab, z, z_lead, W)
        q = _fft_mid(tab, split(y), kspec, order).reshape(batch, 2 * r1, r2, W)
        z, z_lead = _fft_out(tab, q, z, z_lead, uc, (order + 1,), bias[order]), ()
    return z.reshape(T, W)


def kernel(x, c, cond_w, cond_b, mod_w, mod_b, norm_g, ffn_w1, ffn_w3, ffn_w2, ev_in_w, ev_out_w, ret_gn_g, lru_conv_w, lru_conv_b, lru_a_w, lru_a_b, lru_i_w, lru_i_b, lru_lambda, hy_in_w, hy_in_b, hy_conv_w, hy_conv_b, hy_f_w1, hy_f_b1, hy_f_w2, hy_f_b2, hy_f_w3, hy_f_freq, hy_bias, hy_out_w, final_g):
    B, L, D = x.shape
    depth = mod_w.shape[0]
    T = B * L
    mods = _conditioning(c, cond_w, cond_b, mod_w, mod_b)
    tab = _fft_tables(L) if depth > 1 else None
    xt = x.reshape(T, D)

    def ffn(xt, layer, which, mod, idx):
        shift, scale, gate = (mod[:, j:j + 1, :] for j in idx)
        h = _norm_mod(xt, norm_g[layer, 2 * which], scale, shift, L)
        u = _ffn_up(h, ffn_w1, ffn_w3, (layer, which))
        return _res_mm([u], ffn_w2, (layer, which), xt, gate, L, 0.5)

    for layer in range(depth):
        mod = mods[layer]
        xt = ffn(xt, layer, 0, mod, (0, 1, 2))
        h = _norm_mod(xt, norm_g[layer, 1], mod[:, 4:5, :], mod[:, 3:4, :], L)
        gate = mod[:, 5:6, :]
        if layer % 2 == 0:
            e = layer // 2
            RW = ret_gn_g.shape[1]
            u_ret = _matmul(h, ev_in_w, (e,), w_col=0, n_out=4 * RW, out_dtype=BF16)
            u_lru = _matmul(h, ev_in_w, (e,), w_col=4 * RW, out_dtype=F32)
            ret = _retention(u_ret, ret_gn_g[e], B, L, RET_HEADS)
            lru = _lru(u_lru, lru_conv_w[e], lru_conv_b[e], lru_a_w[e], lru_a_b[e], lru_i_w[e], lru_i_b[e],
                       lru_lambda[e], B, L)
            xt = _res_mm([ret, lru], ev_out_w, (e,), xt, gate, L, 1.0)
        else:
            o = layer // 2
            z = _hyena_mix(h, tab, hy_in_w, (o,), hy_in_b[o], hy_conv_w[o], hy_conv_b[o], hy_f_w1[o], hy_f_b1[o],
                           hy_f_w2[o], hy_f_b2[o], hy_f_w3[o], hy_f_freq[o], hy_bias[o], B, L)
            xt = _res_mm([z], hy_out_w, (o,), xt, gate, L, 1.0)
        xt = ffn(xt, layer, 1, mod, (6, 7, 8))
    return _final_norm(xt, final_g).reshape(B, L, D)
```

```python
import functools
import math

import jax
import jax.numpy as jnp
from jax import lax
from jax.experimental import pallas as pl
from jax.experimental.pallas import tpu as pltpu

F32 = jnp.float32
BF16 = jnp.bfloat16

EPS = 1e-6
RET_HEADS = 8
RET_CHUNK = 256
ROPE_BASE = 10000.0
LRU_C = 8.0
LRU_ROWS = 128
HY_FAST_DECAY = 0.3
HY_SLOW_DECAY = 1.5
HY_TARGET = 1e-2
FFT_R2 = 128
SUBLANES = 8
LANES = 128
HALO = SUBLANES
VMEM_LIMIT_BYTES = 62 * 1024 * 1024


def _params(n_axes, vmem=VMEM_LIMIT_BYTES):
    return pltpu.CompilerParams(dimension_semantics=("arbitrary",) * n_axes, vmem_limit_bytes=vmem)


def _bdot(a, b):
    return jnp.dot(a, b, preferred_element_type=F32)


def _norm_mod_kernel(x_ref, g_ref, sc_ref, sh_ref, o_ref):
    x = x_ref[...]
    y = x * lax.rsqrt(jnp.mean(x * x, axis=-1, keepdims=True) + EPS) * g_ref[...]
    o_ref[...] = (y * (1.0 + sc_ref[...]) + sh_ref[...]).astype(o_ref.dtype)


def _norm_kernel(x_ref, g_ref, o_ref):
    x = x_ref[...]
    y = x * lax.rsqrt(jnp.mean(x * x, axis=-1, keepdims=True) + EPS) * g_ref[...]
    o_ref[...] = y.astype(o_ref.dtype)


def _norm_mod(x, g, scale, shift, seq, tm=256):
    T, D = x.shape
    row = pl.BlockSpec((tm, D), lambda i: (i, 0))
    mod = pl.BlockSpec((None, 1, D), lambda i: ((i * tm) // seq, 0, 0))
    return pl.pallas_call(
        _norm_mod_kernel,
        grid=(T // tm,),
        in_specs=[row, pl.BlockSpec((1, D), lambda i: (0, 0)), mod, mod],
        out_specs=row,
        out_shape=jax.ShapeDtypeStruct((T, D), BF16),
        compiler_params=_params(1),
        name="norm_mod",
    )(x, g.reshape(1, D), scale, shift)


def _final_norm(x, g, tm=256):
    T, D = x.shape
    row = pl.BlockSpec((tm, D), lambda i: (i, 0))
    return pl.pallas_call(
        _norm_kernel,
        grid=(T // tm,),
        in_specs=[row, pl.BlockSpec((1, D), lambda i: (0, 0))],
        out_specs=row,
        out_shape=jax.ShapeDtypeStruct((T, D), x.dtype),
        compiler_params=_params(1),
        name="final_norm",
    )(x, g.reshape(1, D))


def _resident(tm, K):
    return pl.BlockSpec((tm, K), lambda m, n: (m, 0))


def _wspec(w_idx, K, tn, row_blk=0, col_blk=0):
    lead = tuple(w_idx)
    return pl.BlockSpec((None,) * len(lead) + (K, tn), lambda m, n: lead + (row_blk, col_blk + n))


def _mm_kernel(*refs, has_bias):
    if has_bias:
        a_ref, w_ref, b_ref, o_ref = refs
    else:
        a_ref, w_ref, o_ref = refs
    acc = _bdot(a_ref[...], w_ref[...].astype(BF16))
    if has_bias:
        acc = acc + b_ref[...]
    o_ref[...] = acc.astype(o_ref.dtype)


def _matmul(a, w, w_idx, bias=None, *, w_col=0, n_out=None, out_dtype, tm=2048, tn=512):
    M, K = a.shape
    N = w.shape[-1] - w_col if n_out is None else n_out
    tm, tn = min(tm, M), min(tn, N)
    in_specs = [_resident(tm, K), _wspec(w_idx, K, tn, col_blk=w_col // tn)]
    args = [a, w]
    if bias is not None:
        in_specs.append(pl.BlockSpec((1, tn), lambda m, n: (0, n)))
        args.append(bias.reshape(1, N))
    return pl.pallas_call(
        functools.partial(_mm_kernel, has_bias=bias is not None),
        grid=(M // tm, N // tn),
        in_specs=in_specs,
        out_specs=pl.BlockSpec((tm, tn), lambda m, n: (m, n)),
        out_shape=jax.ShapeDtypeStruct((M, N), out_dtype),
        compiler_params=_params(2),
        name="matmul",
    )(*args)


def _ffn_up_kernel(h_ref, w1_ref, w3_ref, o_ref):
    h = h_ref[...]
    a = _bdot(h, w1_ref[...].astype(BF16))
    b = _bdot(h, w3_ref[...].astype(BF16))
    o_ref[...] = (a * jax.nn.sigmoid(a) * b).astype(o_ref.dtype)


def _ffn_up(h, w1, w3, w_idx, tm=2048, tn=256):
    M, K = h.shape
    N = w1.shape[-1]
    tm = min(tm, M)
    wspec = _wspec(w_idx, K, tn)
    return pl.pallas_call(
        _ffn_up_kernel,
        grid=(M // tm, N // tn),
        in_specs=[_resident(tm, K), wspec, wspec],
        out_specs=pl.BlockSpec((tm, tn), lambda m, n: (m, n)),
        out_shape=jax.ShapeDtypeStruct((M, N), BF16),
        compiler_params=_params(2),
        name="ffn_up",
    )(h, w1, w3)


def _res_mm_kernel(*refs, n_pairs, coef):
    x_ref, g_ref, o_ref = refs[2 * n_pairs:]
    acc = _bdot(refs[0][...], refs[1][...].astype(BF16))
    for i in range(1, n_pairs):
        acc = acc + _bdot(refs[2 * i][...], refs[2 * i + 1][...].astype(BF16))
    o_ref[...] = x_ref[...] + (coef * (1.0 + g_ref[...])) * acc


def _res_mm(acts, w, w_idx, x, gate, seq, coef, tm=1024, tn=256):
    T, N = x.shape
    tm = min(tm, T, seq)
    in_specs = []
    for i, a in enumerate(acts):
        K = a.shape[1]
        in_specs += [_resident(tm, K), _wspec(w_idx, K, tn, row_blk=i)]
    tile = pl.BlockSpec((tm, tn), lambda m, n: (m, n))
    in_specs += [tile, pl.BlockSpec((None, 1, tn), lambda m, n: ((m * tm) // seq, 0, n))]
    args = [v for a in acts for v in (a, w)]
    return pl.pallas_call(
        functools.partial(_res_mm_kernel, n_pairs=len(acts), coef=coef),
        grid=(T // tm, N // tn),
        in_specs=in_specs,
        out_specs=tile,
        out_shape=jax.ShapeDtypeStruct((T, N), x.dtype),
        compiler_params=_params(2),
        name="res_mm",
    )(*args, x, gate)


def _cond_kernel(c_ref, w_ref, b_ref, o_ref):
    a = _bdot(c_ref[...].astype(BF16), w_ref[...].astype(BF16)) + b_ref[...]
    o_ref[...] = a * jax.nn.sigmoid(a)


def _mod_kernel(h_ref, w_ref, b_ref, o_ref):
    o_ref[...] = _bdot(h_ref[...].astype(BF16), w_ref[...].astype(BF16)) + b_ref[...]


def _conditioning(c, cond_w, cond_b, mod_w, mod_b, tn=2048):
    B, D = c.shape
    tn = min(tn, D)
    H = cond_w.shape[1]
    depth, _, NM = mod_w.shape
    rows = SUBLANES
    c_pad = jnp.zeros((rows, D), c.dtype).at[:B].set(c)
    c_hid = pl.pallas_call(
        _cond_kernel,
        out_shape=jax.ShapeDtypeStruct((rows, H), F32),
        compiler_params=_params(0),
        name="cond",
    )(c_pad, cond_w, cond_b.reshape(1, H))
    mod = pl.pallas_call(
        _mod_kernel,
        grid=(depth, NM // tn),
        in_specs=[
            pl.BlockSpec((rows, H), lambda l, n: (0, 0)),
            pl.BlockSpec((None, H, tn), lambda l, n: (l, 0, n)),
            pl.BlockSpec((None, 1, tn), lambda l, n: (l, 0, n)),
        ],
        out_specs=pl.BlockSpec((None, rows, tn), lambda l, n: (l, 0, n)),
        out_shape=jax.ShapeDtypeStruct((depth, rows, NM), F32),
        compiler_params=_params(2),
        name="mod",
    )(c_hid, mod_w, mod_b.reshape(depth, 1, NM))
    return mod[:, :B].reshape(depth, B, NM // D, D)


def _retention_kernel(lg_ref, q_ref, k_ref, v_ref, g_ref, cos_ref, sin_ref, gn_ref, o_ref,
                      qr, kr, yacc, st, mask_s, xif_s, xib_s, zf_s, zb_s, *, chunk, k_scale):
    L, Dh = q_ref.shape
    half = Dh // 2
    C = chunk
    nc = L // C
    lg = lg_ref[pl.program_id(1)]
    contract_last = (((1,), (1,)), ((), ()))
    contract_rows = (((0,), (0,)), ((), ()))

    def rot_body(i, carry):
        r = pl.ds(pl.multiple_of(i * C, C), C)
        cs = cos_ref[r, :]
        sn = sin_ref[r, :]
        for src, dst, sc in ((q_ref, qr, 1.0), (k_ref, kr, k_scale)):
            x = src[r, :].astype(F32)
            x1 = x[:, :half]
            x2 = x[:, half:]
            y = jnp.concatenate([x1 * cs - x2 * sn, x1 * sn + x2 * cs], axis=-1) * sc
            dst[r, :] = y.astype(BF16)
        return carry

    lax.fori_loop(0, nc, rot_body, 0)

    n_i = lax.broadcasted_iota(jnp.int32, (C, C), 0)
    m_i = lax.broadcasted_iota(jnp.int32, (C, C), 1)
    mask_s[...] = jnp.exp(jnp.abs(n_i - m_i).astype(F32) * lg)
    row = lax.broadcasted_iota(jnp.int32, (C, Dh), 0).astype(F32)
    xif_s[...] = jnp.exp((row + 1.0) * lg)
    xib_s[...] = jnp.exp((C - row) * lg)
    zf_s[...] = jnp.exp((C - 1.0 - row) * lg)
    zb_s[...] = jnp.exp(row * lg)
    decay = jnp.exp(jnp.full((1, Dh), float(C), F32) * lg)

    def chunk_refs(c):
        r = pl.ds(pl.multiple_of(c * C, C), C)
        return r, qr[r, :], kr[r, :], v_ref[r, :]

    def advance(kc, vc, zeta):
        kz = (kc.astype(F32) * zeta).astype(BF16)
        st[...] = st[...] * decay + lax.dot_general(kz, vc, contract_rows, preferred_element_type=F32)

    st[...] = jnp.zeros_like(st)

    def fwd_body(c, carry):
        r, qc, kc, vc = chunk_refs(c)
        s = lax.dot_general(qc, kc, contract_last, preferred_element_type=F32) * mask_s[...]
        y = _bdot(s.astype(BF16), vc)
        y = y + _bdot(qc, st[...].astype(BF16)) * xif_s[...]
        yacc[r, :] = y
        advance(kc, vc, zf_s[...])
        return carry

    lax.fori_loop(0, nc, fwd_body, 0)

    st[...] = jnp.zeros_like(st)

    def bwd_body(i, carry):
        r, qc, kc, vc = chunk_refs(nc - 1 - i)
        y = yacc[r, :] + _bdot(qc, st[...].astype(BF16)) * xib_s[...]
        mu = jnp.mean(y, axis=-1, keepdims=True)
        d = y - mu
        var = jnp.mean(d * d, axis=-1, keepdims=True)
        yn = d * lax.rsqrt(var + EPS) * gn_ref[...]
        gg = g_ref[r, :].astype(F32)
        o_ref[r, :] = (gg * jax.nn.sigmoid(gg) * yn).astype(o_ref.dtype)
        advance(kc, vc, zb_s[...])
        return carry

    lax.fori_loop(0, nc, bwd_body, 0)


def _retention(u_ret, gn_g, batch, seq, n_heads, chunk=RET_CHUNK):
    T, W4 = u_ret.shape
    RW = W4 // 4
    Dh = RW // n_heads
    half = Dh // 2
    H = n_heads
    chunk = min(chunk, seq)
    log_g = jnp.log1p(-(2.0 ** (-5.0 - jnp.arange(H, dtype=F32))))
    inv = ROPE_BASE ** (-jnp.arange(half, dtype=F32) / half)
    ang = jnp.arange(seq, dtype=F32)[:, None] * inv[None, :]
    cos, sin = jnp.cos(ang), jnp.sin(ang)

    def col(off):
        return pl.BlockSpec((seq, Dh), lambda b, h: (b, off + h))

    table = pl.BlockSpec((seq, half), lambda b, h: (0, 0))
    return pl.pallas_call(
        functools.partial(_retention_kernel, chunk=chunk, k_scale=Dh ** -0.5),
        grid=(batch, H),
        in_specs=[pl.BlockSpec(memory_space=pltpu.SMEM), col(0), col(H), col(2 * H), col(3 * H), table, table,
                  pl.BlockSpec((1, Dh), lambda b, h: (0, h))],
        out_specs=pl.BlockSpec((seq, Dh), lambda b, h: (b, h)),
        out_shape=jax.ShapeDtypeStruct((T, RW), BF16),
        scratch_shapes=[
            pltpu.VMEM((seq, Dh), BF16), pltpu.VMEM((seq, Dh), BF16), pltpu.VMEM((seq, Dh), F32),
            pltpu.VMEM((Dh, Dh), F32), pltpu.VMEM((chunk, chunk), F32),
            pltpu.VMEM((chunk, Dh), F32), pltpu.VMEM((chunk, Dh), F32),
            pltpu.VMEM((chunk, Dh), F32), pltpu.VMEM((chunk, Dh), F32),
        ],
        compiler_params=_params(2),
        name="retention",
    )(log_g, u_ret, u_ret, u_ret, u_ret, cos, sin, gn_g.reshape(1, RW))


def _fill_padded(src_ref, xp, rows):
    L, n = src_ref.shape
    zeros = jnp.zeros((HALO, n), xp.dtype)
    xp[0:HALO, :] = zeros
    xp[HALO + L:HALO + L + HALO, :] = zeros

    def body(i, carry):
        r0 = pl.multiple_of(i * rows, rows)
        xp[pl.ds(r0 + HALO, rows), :] = src_ref[pl.ds(r0, rows), :].astype(xp.dtype)
        return carry

    lax.fori_loop(0, L // rows, body, 0)


def _taps(xe, w_ref, b_ref, rows, first):
    acc = b_ref[...]
    for k in range(w_ref.shape[0]):
        o = HALO + first + k
        acc = acc + w_ref[k:k + 1, :] * xe[o:o + rows, :]
    return acc


def _lru_kernel(xr_ref, xg_ref, cw_ref, cb_ref, w_ref, b_ref, lam_ref, o_ref, xp, stage, at, bt, *, rows):
    L, n = xr_ref.shape
    R = rows
    NS = SUBLANES
    S = L // NS
    nj = S // R
    _fill_padded(xr_ref, xp, R)
    nl = -lam_ref[...]
    c8 = -LRU_C * (jnp.maximum(nl, 0.0) + jnp.log1p(jnp.exp(-jnp.abs(nl))))

    def gate_body(jc, carry):
        j0 = pl.multiple_of(jc * R, R)

        def seg_body(s, c):
            r0 = pl.multiple_of(s * S + j0, R)
            xe = xp[pl.ds(r0, R + 2 * HALO), :]
            xc = _taps(xe, cw_ref, cb_ref, R, -2)
            pre = _bdot(xc.astype(BF16), w_ref[...]) + b_ref[...]
            sig = 0.5 * jnp.tanh(0.5 * pre) + 0.5
            for d in range(2):
                r = sig[:, 2 * n * d:2 * n * d + n]
                ig = sig[:, 2 * n * d + n:2 * n * (d + 1)]
                log_a = c8[d:d + 1, :] * r
                a = jnp.exp(log_a)
                stage[2 * d, s] = a
                stage[2 * d + 1, s] = jnp.sqrt(-jnp.tanh(log_a) * (a * a + 1.0)) * (ig * xc)
            return c

        lax.fori_loop(0, NS, seg_body, 0)
        for d in range(2):
            at[d, pl.ds(j0, R)] = pltpu.einshape("sjc->jsc", stage[2 * d])
            bt[d, pl.ds(j0, R)] = pltpu.einshape("sjc->jsc", stage[2 * d + 1])
        return carry

    lax.fori_loop(0, nj, gate_body, 0)

    def totals_body(j, carry):
        h0, p0, h1, p1 = carry
        jr = S - 1 - j
        a0, a1 = at[0, j], at[1, jr]
        return a0 * h0 + bt[0, j], a0 * p0, a1 * h1 + bt[1, jr], a1 * p1

    zero = jnp.zeros((NS, n), F32)
    one = jnp.ones((NS, n), F32)
    h0, p0, h1, p1 = lax.fori_loop(0, S, totals_body, (zero, one, zero, one), unroll=8)

    row = jnp.zeros((1, n), F32)
    rows0 = [row]
    for s in range(1, NS):
        row = h0[s - 1:s] + p0[s - 1:s] * row
        rows0.append(row)
    row = jnp.zeros((1, n), F32)
    rows1 = [row]
    for s in range(NS - 2, -1, -1):
        row = h1[s + 1:s + 2] + p1[s + 1:s + 2] * row
        rows1.insert(0, row)

    def final_body(j, carry):
        h0, h1 = carry
        jr = S - 1 - j
        h0 = at[0, j] * h0 + bt[0, j]
        h1 = at[1, jr] * h1 + bt[1, jr]
        bt[0, j] = h0
        bt[1, jr] = h1
        return h0, h1

    lax.fori_loop(0, S, final_body, (jnp.concatenate(rows0, axis=0), jnp.concatenate(rows1, axis=0)), unroll=8)

    def out_body(jc, carry):
        j0 = pl.multiple_of(jc * R, R)
        hs = pltpu.einshape("jsc->sjc", bt[0, pl.ds(j0, R)] + bt[1, pl.ds(j0, R)])
        for s in range(NS):
            r = pl.ds(pl.multiple_of(s * S + j0, R), R)
            xg = xg_ref[r, :]
            gelu = 0.5 * xg * (1.0 + jnp.tanh(math.sqrt(2.0 / math.pi) * (xg + 0.044715 * (xg * xg * xg))))
            o_ref[r, :] = (hs[s] * gelu).astype(o_ref.dtype)
        return carry

    lax.fori_loop(0, nj, out_body, 0)


def _lru(u_lru, conv_w, conv_b, a_w, a_b, i_w, i_b, lam, batch, seq, rows=LRU_ROWS):
    T, W2 = u_lru.shape
    W = W2 // 2
    nb, n = a_w.shape[1], a_w.shape[2]
    rows = min(rows, seq // SUBLANES)
    w_cat = jnp.concatenate([a_w[0], i_w[0], a_w[1], i_w[1]], axis=-1).astype(BF16)
    b_cat = jnp.stack([a_b[0], i_b[0], a_b[1], i_b[1]], axis=0).reshape(4, nb, n).transpose(1, 0, 2).reshape(nb, 1, 4 * n)
    return pl.pallas_call(
        functools.partial(_lru_kernel, rows=rows),
        grid=(batch, nb),
        in_specs=[
            pl.BlockSpec((seq, n), lambda b, j: (b, j)),
            pl.BlockSpec((seq, n), lambda b, j: (b, nb + j)),
            pl.BlockSpec((conv_w.shape[0], n), lambda b, j: (0, j)),
            pl.BlockSpec((1, n), lambda b, j: (0, j)),
            pl.BlockSpec((None, n, 4 * n), lambda b, j: (j, 0, 0)),
            pl.BlockSpec((None, 1, 4 * n), lambda b, j: (j, 0, 0)),
            pl.BlockSpec((2, n), lambda b, j: (0, j)),
        ],
        out_specs=pl.BlockSpec((seq, n), lambda b, j: (b, j)),
        out_shape=jax.ShapeDtypeStruct((T, W), BF16),
        scratch_shapes=[pltpu.VMEM((seq + 2 * HALO, n), F32), pltpu.VMEM((4, SUBLANES, rows, n), F32),
                        pltpu.VMEM((2, seq // SUBLANES, SUBLANES, n), F32),
                        pltpu.VMEM((2, seq // SUBLANES, SUBLANES, n), F32)],
        compiler_params=_params(2),
        name="rglru",
    )(u_lru, u_lru, conv_w, conv_b.reshape(1, W), w_cat, b_cat, lam)


def _conv3_kernel(u_ref, w_ref, b_ref, o_ref, xp, *, rows):
    L, n = u_ref.shape
    _fill_padded(u_ref, xp, rows)

    def body(i, carry):
        r0 = pl.multiple_of(i * rows, rows)
        xe = xp[pl.ds(r0, rows + 2 * HALO), :]
        o_ref[pl.ds(r0, rows), :] = _taps(xe, w_ref, b_ref, rows, -1).astype(o_ref.dtype)
        return carry

    lax.fori_loop(0, L // rows, body, 0)


def _conv3(u, w, b, batch, seq, parts=3, tn=512, rows=256):
    T, N = u.shape
    W = N // parts
    rows = min(rows, seq)
    nw = W // tn
    return pl.pallas_call(
        functools.partial(_conv3_kernel, rows=rows),
        grid=(batch, parts, nw),
        in_specs=[pl.BlockSpec((seq, tn), lambda bi, p, j: (bi, p * nw + j)),
                  pl.BlockSpec((w.shape[0], tn), lambda bi, p, j: (0, p * nw + j)),
                  pl.BlockSpec((1, tn), lambda bi, p, j: (0, p * nw + j))],
        out_specs=pl.BlockSpec((None, seq, tn), lambda bi, p, j: (p, bi, j)),
        out_shape=jax.ShapeDtypeStruct((parts, T, W), BF16),
        scratch_shapes=[pltpu.VMEM((seq + 2 * HALO, tn), F32)],
        compiler_params=_params(3),
        name="conv3",
    )(u, w, b.reshape(1, N))


def _filter_kernel(z_ref, w1_ref, b1_ref, w2_ref, b2_ref, fr_ref, wf_ref, wb_ref, dl_ref, e_ref, o_ref, hid, *, rows):
    L = z_ref.shape[0]
    hi = lax.Precision.HIGHEST

    @pl.when((pl.program_id(0) == 0) & (pl.program_id(1) == 0))
    def _():
        def body(i, carry):
            r = pl.ds(pl.multiple_of(i * rows, rows), rows)
            h1 = jnp.sin(fr_ref[0:1, :] * (jnp.dot(z_ref[r, :], w1_ref[...], precision=hi, preferred_element_type=F32) + b1_ref[...]))
            h2 = jnp.sin(fr_ref[1:2, :] * (jnp.dot(h1, w2_ref[...], precision=hi, preferred_element_type=F32) + b2_ref[...]))
            hid[r, :] = h2
            return carry

        lax.fori_loop(0, L // rows, body, 0)

    def body(i, carry):
        r0 = pl.multiple_of(i * rows, rows)
        r = pl.ds(r0, rows)
        hh = hid[r, :].astype(BF16)
        t = z_ref[r, 0:1]
        win = jnp.exp(-t * jnp.abs(dl_ref[...]))
        hf = _bdot(hh, wf_ref[...].astype(BF16)) * win
        hb = _bdot(hh, wb_ref[...].astype(BF16)) * win
        lag = lax.broadcasted_iota(jnp.int32, hb.shape, 0) + r0
        hb = jnp.where(lag == 0, 0.0, hb)
        e_ref[r, :] = (hf + hb).astype(e_ref.dtype)
        o_ref[r, :] = (hf - hb).astype(o_ref.dtype)
        return carry

    lax.fori_loop(0, L // rows, body, 0)


def _hyena_filters(seq, width, w1, b1, w2, b2, w3, freq, tn=512, rows=512):
    emb, hid_in = w1.shape
    order = w3.shape[1] // (2 * width)
    rows = min(rows, seq)
    tn = min(tn, width)
    nt = width // tn
    t = jnp.linspace(0.0, 1.0, seq, dtype=F32)[:, None]
    bands = (emb - 1) // 2
    f = jnp.linspace(1e-4, bands - 1, bands, dtype=F32)[None, :]
    w = 2.0 * math.pi * jnp.arange(seq, dtype=F32)[:, None] / seq
    z = jnp.concatenate([t, jnp.cos(f * w), -jnp.sin(f * w)], axis=-1)
    hidden = LANES
    pad = lambda a, shape: jnp.zeros(shape, F32).at[tuple(slice(0, d) for d in a.shape)].set(a)
    zp = pad(z, (seq, LANES))
    w1p = pad(w1, (LANES, hidden))
    b1, b2 = pad(b1[None], (1, hidden)), pad(b2[None], (1, hidden))
    w2 = pad(w2, (hidden, hidden))
    w3 = pad(w3, (hidden, w3.shape[1]))
    freq = pad(freq, (2, hidden))
    deltas = jnp.linspace(math.log(HY_TARGET) / HY_SLOW_DECAY, math.log(HY_TARGET) / HY_FAST_DECAY, width, dtype=F32)
    full = lambda shape: pl.BlockSpec(shape, lambda o, j: (0,) * len(shape))
    out = pl.BlockSpec((None, seq, tn), lambda o, j: (o, 0, j))
    return pl.pallas_call(
        functools.partial(_filter_kernel, rows=rows),
        grid=(order, nt),
        in_specs=[
            full((seq, LANES)), full((LANES, hidden)), full((1, hidden)), full((hidden, hidden)), full((1, hidden)),
            full((2, hidden)),
            pl.BlockSpec((hidden, tn), lambda o, j: (0, o * 2 * nt + j)),
            pl.BlockSpec((hidden, tn), lambda o, j: (0, o * 2 * nt + nt + j)),
            pl.BlockSpec((1, tn), lambda o, j: (0, j)),
        ],
        out_specs=[out, out],
        out_shape=[jax.ShapeDtypeStruct((order, seq, width), BF16)] * 2,
        scratch_shapes=[pltpu.VMEM((seq, hidden), F32)],
        compiler_params=_params(2),
        name="hyena_filter",
    )(zp, w1p, b1.reshape(1, hidden), w2, b2.reshape(1, hidden), freq, w3, w3, deltas.reshape(1, width))


def _fft_tables(seq, r2=FFT_R2):
    r2 = min(r2, seq // 16)
    r1 = seq // r2
    a1 = jnp.arange(r1, dtype=jnp.int32)
    a2 = jnp.arange(r2, dtype=jnp.int32)
    ang1 = (((2 * a1[:, None] + 1) * a1[None, :]) % (4 * r1)).astype(F32) * (2.0 * math.pi / (4 * r1))
    c1, s1 = jnp.cos(ang1), jnp.sin(ang1)
    fwd1 = jnp.concatenate([c1, -s1], axis=0).astype(BF16)
    inv1 = jnp.concatenate([c1.T, -s1.T], axis=1).astype(BF16)
    k = a1[:, None, None] + 2 * r1 * a2[None, :, None]
    ang2 = (((2 * k + 1) * a2[None, None, :]) % (4 * seq)).astype(F32) * (2.0 * math.pi / (4 * seq))
    c2, s2 = jnp.cos(ang2), jnp.sin(ang2)
    fwd2 = jnp.concatenate([jnp.concatenate([c2, s2], axis=2), jnp.concatenate([-s2, c2], axis=2)], axis=1)
    fwd2 = fwd2.astype(BF16)
    return dict(r1=r1, r2=r2, fwd1=fwd1, inv1=inv1, fwd2=fwd2, inv2=jnp.swapaxes(fwd2, 1, 2))


def _time_spec(lead, r1, n, tn):
    lead = tuple(lead)
    return pl.BlockSpec((None,) * (len(lead) + 1) + (r1, n, tn), lambda b, s, j: lead + (b, 0, s, j))


def _fft_s1_kernel(m_ref, x_ref, o_ref):
    n = x_ref.shape[1]
    xt = pltpu.einshape("tjc->jtc", x_ref[...])
    y = jnp.stack([_bdot(m_ref[...], xt[j]) for j in range(n)], axis=0).astype(o_ref.dtype)
    o_ref[...] = pltpu.einshape("jkc->kjc", y)


def _fft_stage1(tab, x, lead, width, n=16, tn=1024):
    r1, r2 = tab["r1"], tab["r2"]
    B = x.shape[len(lead)]
    n, tn = min(n, r2), min(tn, width)
    return pl.pallas_call(
        _fft_s1_kernel,
        grid=(B, r2 // n, width // tn),
        in_specs=[pl.BlockSpec((2 * r1, r1), lambda b, s, j: (0, 0)), _time_spec(lead, r1, n, tn)],
        out_specs=_time_spec((), 2 * r1, n, tn),
        out_shape=jax.ShapeDtypeStruct((B, 2 * r1, r2, width), BF16),
        compiler_params=_params(3),
        name="hyena_fft_stage1",
    )(tab["fwd1"], x)


def _k1_specs(r2, n, tn):
    return [pl.BlockSpec((None, 2, None, r2, tn), lambda b, s, j, i=i: (b, 0, s * n + i, 0, j)) for i in range(n)]


def _fft_spec_kernel(a_ref, *refs, n):
    ye, yo, o_ref = refs[:n], refs[n:2 * n], refs[2 * n]
    r2 = a_ref.shape[1] // 2
    for i in range(n):
        xe = _bdot(a_ref[i], ye[i][...].reshape(2 * r2, -1))
        xo = _bdot(a_ref[i], yo[i][...].reshape(2 * r2, -1))
        o_ref[i, :r2, :] = xe[:r2].astype(o_ref.dtype)
        o_ref[i, r2:, :] = xo[r2:].astype(o_ref.dtype)


def _fft_filter_spectrum(tab, ye, yo, n=8, tn=1024):
    r1, r2 = tab["r1"], tab["r2"]
    orders, W = ye.shape[0], ye.shape[-1]
    n, tn = min(n, r1), min(tn, W)
    return pl.pallas_call(
        functools.partial(_fft_spec_kernel, n=n),
        grid=(orders, r1 // n, W // tn),
        in_specs=[pl.BlockSpec((n, 2 * r2, 2 * r2), lambda b, s, j: (s, 0, 0))] + _k1_specs(r2, n, tn) * 2,
        out_specs=pl.BlockSpec((None, n, 2 * r2, tn), lambda b, s, j: (b, s, 0, j)),
        out_shape=jax.ShapeDtypeStruct((orders, r1, 2 * r2, W), BF16),
        compiler_params=_params(3),
        name="hyena_fft_spectrum",
    )(tab["fwd2"], *([ye] * n), *([yo] * n))


def _fft_mid_kernel(a_ref, at_ref, k_ref, *refs, n, scale):
    ys, o_ref = refs[:n], refs[n]
    r2 = a_ref.shape[1] // 2
    for i in range(n):
        x = _bdot(a_ref[i], ys[i][...].reshape(2 * r2, -1))
        xr, xi = x[:r2], x[r2:]
        kr = k_ref[i, :r2, :].astype(F32) * scale
        ki = k_ref[i, r2:, :].astype(F32) * scale
        p = jnp.concatenate([xr * kr - xi * ki, xr * ki + xi * kr], axis=0).astype(BF16)
        q = _bdot(at_ref[i], p).astype(o_ref.dtype)
        o_ref[0, i] = q[:r2]
        o_ref[1, i] = q[r2:]


def _fft_mid(tab, y, kspec, order, n=8, tn=1024):
    r1, r2 = tab["r1"], tab["r2"]
    B, W = y.shape[0], y.shape[-1]
    n, tn = min(n, r1), min(tn, W)
    mat = pl.BlockSpec((n, 2 * r2, 2 * r2), lambda s, j, b: (s, 0, 0))
    y_specs = [pl.BlockSpec((None, 2, None, r2, tn), lambda s, j, b, i=i: (b, 0, s * n + i, 0, j)) for i in range(n)]
    return pl.pallas_call(
        functools.partial(_fft_mid_kernel, n=n, scale=1.0 / (r1 * r2)),
        grid=(r1 // n, W // tn, B),
        in_specs=[mat, mat, pl.BlockSpec((None, n, 2 * r2, tn), lambda s, j, b: (order, s, 0, j))] + y_specs,
        out_specs=pl.BlockSpec((None, 2, n, r2, tn), lambda s, j, b: (b, 0, s, 0, j)),
        out_shape=jax.ShapeDtypeStruct((B, 2, r1, r2, W), BF16),
        compiler_params=_params(3),
        name="hyena_fft_mid",
    )(tab["fwd2"], tab["inv2"], kspec, *([y] * n))


def _fft_out_kernel(m_ref, bias_ref, q_ref, x_ref, g_ref, o_ref):
    n = q_ref.shape[1]
    qt = pltpu.einshape("kjc->jkc", q_ref[...])
    y = jnp.stack([_bdot(m_ref[...], qt[j]) for j in range(n)], axis=0)
    y = pltpu.einshape("jtc->tjc", y)
    x = x_ref[...].astype(F32)
    o_ref[...] = (g_ref[...].astype(F32) * (y + bias_ref[...] * x)).astype(o_ref.dtype)


def _fft_out(tab, q, x, x_lead, gate, gate_lead, bias, n=16, tn=1024):
    r1, r2 = tab["r1"], tab["r2"]
    B = q.shape[0]
    W = bias.shape[0]
    n, tn = min(n, r2), min(tn, W)
    return pl.pallas_call(
        _fft_out_kernel,
        grid=(B, r2 // n, W // tn),
        in_specs=[pl.BlockSpec((r1, 2 * r1), lambda b, s, j: (0, 0)), pl.BlockSpec((1, tn), lambda b, s, j: (0, j)),
                  _time_spec((), 2 * r1, n, tn), _time_spec(x_lead, r1, n, tn), _time_spec(gate_lead, r1, n, tn)],
        out_specs=_time_spec((), r1, n, tn),
        out_shape=jax.ShapeDtypeStruct((B, r1, r2, W), BF16),
        compiler_params=_params(3),
        name="hyena_fft_out",
    )(tab["inv1"], bias.reshape(1, W), q, x, gate)


def _hyena_mix(h, tab, w_in, w_idx, b_in, conv_w, conv_b, f_w1, f_b1, f_w2, f_b2, f_w3, f_freq, bias, batch, seq):
    T = h.shape[0]
    W = w_in.shape[-1] // 3
    r1, r2 = tab["r1"], tab["r2"]
    u = _matmul(h, w_in, w_idx, b_in, out_dtype=BF16)
    uc = _conv3(u, conv_w, conv_b, batch, seq).reshape(3, batch, r1, r2, W)
    e, o = _hyena_filters(seq, W, f_w1, f_b1, f_w2, f_b2, f_w3, f_freq)
    orders = e.shape[0]
    split = lambda y: y.reshape(y.shape[0], 2, r1, r2, W)
    ye = _fft_stage1(tab, e.reshape(orders, r1, r2, W), (), W)
    yo = _fft_stage1(tab, o.reshape(orders, r1, r2, W), (), W)
    kspec = _fft_filter_spectrum(tab, split(ye), split(yo))
    z, z_lead = uc, (0,)
    for order in range(orders):
        y = _fft_stage1(tab, z, z_lead, W)
        q = _fft_mid(tab, split(y), kspec, order).reshape(batch, 2 * r1, r2, W)
        z, z_lead = _fft_out(tab, q, z, z_lead, uc, (order + 1,), bias[order]), ()
    return z.reshape(T, W)


def kernel(x, c, cond_w, cond_b, mod_w, mod_b, norm_g, ffn_w1, ffn_w3, ffn_w2, ev_in_w, ev_out_w, ret_gn_g, lru_conv_w, lru_conv_b, lru_a_w, lru_a_b, lru_i_w, lru_i_b, lru_lambda, hy_in_w, hy_in_b, hy_conv_w, hy_conv_b, hy_f_w1, hy_f_b1, hy_f_w2, hy_f_b2, hy_f_w3, hy_f_freq, hy_bias, hy_out_w, final_g):
    B, L, D = x.shape
    depth = mod_w.shape[0]
    T = B * L
    mods = _conditioning(c, cond_w, cond_b, mod_w, mod_b)
    tab = _fft_tables(L) if depth > 1 else None
    xt = x.reshape(T, D)

    def ffn(xt, layer, which, mod, idx):
        shift, scale, gate = (mod[:, j:j + 1, :] for j in idx)
        h = _norm_mod(xt, norm_g[layer, 2 * which], scale, shift, L)
        u = _ffn_up(h, ffn_w1, ffn_w3, (layer, which))
        return _res_mm([u], ffn_w2, (layer, which), xt, gate, L, 0.5)

    for layer in range(depth):
        mod = mods[layer]
        xt = ffn(xt, layer, 0, mod, (0, 1, 2))
        h = _norm_mod(xt, norm_g[layer, 1], mod[:, 4:5, :], mod[:, 3:4, :], L)
        gate = mod[:, 5:6, :]
        if layer % 2 == 0:
            e = layer // 2
            RW = ret_gn_g.shape[1]
            u_ret = _matmul(h, ev_in_w, (e,), w_col=0, n_out=4 * RW, out_dtype=BF16)
            u_lru = _matmul(h, ev_in_w, (e,), w_col=4 * RW, out_dtype=F32)
            ret = _retention(u_ret, ret_gn_g[e], B, L, RET_HEADS)
            lru = _lru(u_lru, lru_conv_w[e], lru_conv_b[e], lru_a_w[e], lru_a_b[e], lru_i_w[e], lru_i_b[e],
                       lru_lambda[e], B, L)
            xt = _res_mm([ret, lru], ev_out_w, (e,), xt, gate, L, 1.0)
        else:
            o = layer // 2
            z = _hyena_mix(h, tab, hy_in_w, (o,), hy_in_b[o], hy_conv_w[o], hy_conv_b[o], hy_f_w1[o], hy_f_b1[o],
                           hy_f_w2[o], hy_f_b2[o], hy_f_w3[o], hy_f_freq[o], hy_bias[o], B, L)
            xt = _res_mm([z], hy_out_w, (o,), xt, gate, L, 1.0)
        xt = ffn(xt, layer, 1, mod, (6, 7, 8))
    return _final_norm(xt, final_g).reshape(B, L, D)
```

```python
import functools
import math

import jax
import jax.numpy as jnp
from jax import lax
from jax.experimental import pallas as pl
from jax.experimental.pallas import tpu as pltpu

F32 = jnp.float32
BF16 = jnp.bfloat16

EPS = 1e-6
RET_HEADS = 8
RET_CHUNK = 256
ROPE_BASE = 10000.0
LRU_C = 8.0
LRU_ROWS = 128
HY_FAST_DECAY = 0.3
HY_SLOW_DECAY = 1.5
HY_TARGET = 1e-2
FFT_R2 = 128
SUBLANES = 8
LANES = 128
HALO = SUBLANES
VMEM_LIMIT_BYTES = 62 * 1024 * 1024


def _params(n_axes, vmem=VMEM_LIMIT_BYTES):
    return pltpu.CompilerParams(dimension_semantics=("arbitrary",) * n_axes, vmem_limit_bytes=vmem)


def _bdot(a, b):
    return jnp.dot(a, b, preferred_element_type=F32)


def _norm_mod_kernel(x_ref, g_ref, sc_ref, sh_ref, o_ref):
    x = x_ref[...]
    y = x * lax.rsqrt(jnp.mean(x * x, axis=-1, keepdims=True) + EPS) * g_ref[...]
    o_ref[...] = (y * (1.0 + sc_ref[...]) + sh_ref[...]).astype(o_ref.dtype)


def _norm_kernel(x_ref, g_ref, o_ref):
    x = x_ref[...]
    y = x * lax.rsqrt(jnp.mean(x * x, axis=-1, keepdims=True) + EPS) * g_ref[...]
    o_ref[...] = y.astype(o_ref.dtype)


def _norm_mod(x, g, scale, shift, seq, tm=256):
    T, D = x.shape
    row = pl.BlockSpec((tm, D), lambda i: (i, 0))
    mod = pl.BlockSpec((None, 1, D), lambda i: ((i * tm) // seq, 0, 0))
    return pl.pallas_call(
        _norm_mod_kernel,
        grid=(T // tm,),
        in_specs=[row, pl.BlockSpec((1, D), lambda i: (0, 0)), mod, mod],
        out_specs=row,
        out_shape=jax.ShapeDtypeStruct((T, D), BF16),
        compiler_params=_params(1),
        name="norm_mod",
    )(x, g.reshape(1, D), scale, shift)


def _final_norm(x, g, tm=256):
    T, D = x.shape
    row = pl.BlockSpec((tm, D), lambda i: (i, 0))
    return pl.pallas_call(
        _norm_kernel,
        grid=(T // tm,),
        in_specs=[row, pl.BlockSpec((1, D), lambda i: (0, 0))],
        out_specs=row,
        out_shape=jax.ShapeDtypeStruct((T, D), x.dtype),
        compiler_params=_params(1),
        name="final_norm",
    )(x, g.reshape(1, D))


def _resident(tm, K):
    return pl.BlockSpec((tm, K), lambda m, n: (m, 0))


def _wspec(w_idx, K, tn, row_blk=0, col_blk=0):
    lead = tuple(w_idx)
    return pl.BlockSpec((None,) * len(lead) + (K, tn), lambda m, n: lead + (row_blk, col_blk + n))


def _mm_kernel(*refs, has_bias):
    if has_bias:
        a_ref, w_ref, b_ref, o_ref = refs
    else:
        a_ref, w_ref, o_ref = refs
    acc = _bdot(a_ref[...], w_ref[...].astype(BF16))
    if has_bias:
        acc = acc + b_ref[...]
    o_ref[...] = acc.astype(o_ref.dtype)


def _matmul(a, w, w_idx, bias=None, *, w_col=0, n_out=None, out_dtype, tm=2048, tn=512):
    M, K = a.shape
    N = w.shape[-1] - w_col if n_out is None else n_out
    tm, tn = min(tm, M), min(tn, N)
    in_specs = [_resident(tm, K), _wspec(w_idx, K, tn, col_blk=w_col // tn)]
    args = [a, w]
    if bias is not None:
        in_specs.append(pl.BlockSpec((1, tn), lambda m, n: (0, n)))
        args.append(bias.reshape(1, N))
    return pl.pallas_call(
        functools.partial(_mm_kernel, has_bias=bias is not None),
        grid=(M // tm, N // tn),
        in_specs=in_specs,
        out_specs=pl.BlockSpec((tm, tn), lambda m, n: (m, n)),
        out_shape=jax.ShapeDtypeStruct((M, N), out_dtype),
        compiler_params=_params(2),
        name="matmul",
    )(*args)


def _ffn_up_kernel(h_ref, w1_ref, w3_ref, o_ref):
    h = h_ref[...]
    a = _bdot(h, w1_ref[...].astype(BF16))
    b = _bdot(h, w3_ref[...].astype(BF16))
    o_ref[...] = (a * jax.nn.sigmoid(a) * b).astype(o_ref.dtype)


def _ffn_up(h, w1, w3, w_idx, tm=2048, tn=256):
    M, K = h.shape
    N = w1.shape[-1]
    tm = min(tm, M)
    wspec = _wspec(w_idx, K, tn)
    return pl.pallas_call(
        _ffn_up_kernel,
        grid=(M // tm, N // tn),
        in_specs=[_resident(tm, K), wspec, wspec],
        out_specs=pl.BlockSpec((tm, tn), lambda m, n: (m, n)),
        out_shape=jax.ShapeDtypeStruct((M, N), BF16),
        compiler_params=_params(2),
        name="ffn_up",
    )(h, w1, w3)


def _res_mm_kernel(*refs, n_pairs, coef):
    x_ref, g_ref, o_ref = refs[2 * n_pairs:]
    acc = _bdot(refs[0][...], refs[1][...].astype(BF16))
    for i in range(1, n_pairs):
        acc = acc + _bdot(refs[2 * i][...], refs[2 * i + 1][...].astype(BF16))
    o_ref[...] = x_ref[...] + (coef * (1.0 + g_ref[...])) * acc


def _res_mm(acts, w, w_idx, x, gate, seq, coef, tm=1024, tn=256):
    T, N = x.shape
    tm = min(tm, T, seq)
    in_specs = []
    for i, a in enumerate(acts):
        K = a.shape[1]
        in_specs += [_resident(tm, K), _wspec(w_idx, K, tn, row_blk=i)]
    tile = pl.BlockSpec((tm, tn), lambda m, n: (m, n))
    in_specs += [tile, pl.BlockSpec((None, 1, tn), lambda m, n: ((m * tm) // seq, 0, n))]
    args = [v for a in acts for v in (a, w)]
    return pl.pallas_call(
        functools.partial(_res_mm_kernel, n_pairs=len(acts), coef=coef),
        grid=(T // tm, N // tn),
        in_specs=in_specs,
        out_specs=tile,
        out_shape=jax.ShapeDtypeStruct((T, N), x.dtype),
        compiler_params=_params(2),
        name="res_mm",
    )(*args, x, gate)


def _cond_kernel(c_ref, w_ref, b_ref, o_ref):
    a = _bdot(c_ref[...].astype(BF16), w_ref[...].astype(BF16)) + b_ref[...]
    o_ref[...] = a * jax.nn.sigmoid(a)


def _mod_kernel(h_ref, w_ref, b_ref, o_ref):
    o_ref[...] = _bdot(h_ref[...].astype(BF16), w_ref[...].astype(BF16)) + b_ref[...]


def _conditioning(c, cond_w, cond_b, mod_w, mod_b, tn=2048):
    B, D = c.shape
    tn = min(tn, D)
    H = cond_w.shape[1]
    depth, _, NM = mod_w.shape
    rows = SUBLANES
    c_pad = jnp.zeros((rows, D), c.dtype).at[:B].set(c)
    c_hid = pl.pallas_call(
        _cond_kernel,
        out_shape=jax.ShapeDtypeStruct((rows, H), F32),
        compiler_params=_params(0),
        name="cond",
    )(c_pad, cond_w, cond_b.reshape(1, H))
    mod = pl.pallas_call(
        _mod_kernel,
        grid=(depth, NM // tn),
        in_specs=[
            pl.BlockSpec((rows, H), lambda l, n: (0, 0)),
            pl.BlockSpec((None, H, tn), lambda l, n: (l, 0, n)),
            pl.BlockSpec((None, 1, tn), lambda l, n: (l, 0, n)),
        ],
        out_specs=pl.BlockSpec((None, rows, tn), lambda l, n: (l, 0, n)),
        out_shape=jax.ShapeDtypeStruct((depth, rows, NM), F32),
        compiler_params=_params(2),
        name="mod",
    )(c_hid, mod_w, mod_b.reshape(depth, 1, NM))
    return mod[:, :B].reshape(depth, B, NM // D, D)


def _retention_kernel(lg_ref, q_ref, k_ref, v_ref, g_ref, cos_ref, sin_ref, gn_ref, o_ref,
                      qr, kr, yacc, st, mask_s, xif_s, xib_s, zf_s, zb_s, *, chunk, k_scale):
    L, Dh = q_ref.shape
    half = Dh // 2
    C = chunk
    nc = L // C
    lg = lg_ref[pl.program_id(1)]
    contract_last = (((1,), (1,)), ((), ()))
    contract_rows = (((0,), (0,)), ((), ()))

    def rot_body(i, carry):
        r = pl.ds(pl.multiple_of(i * C, C), C)
        cs = cos_ref[r, :]
        sn = sin_ref[r, :]
        for src, dst, sc in ((q_ref, qr, 1.0), (k_ref, kr, k_scale)):
            x = src[r, :].astype(F32)
            x1 = x[:, :half]
            x2 = x[:, half:]
            y = jnp.concatenate([x1 * cs - x2 * sn, x1 * sn + x2 * cs], axis=-1) * sc
            dst[r, :] = y.astype(BF16)
        return carry

    lax.fori_loop(0, nc, rot_body, 0)

    n_i = lax.broadcasted_iota(jnp.int32, (C, C), 0)
    m_i = lax.broadcasted_iota(jnp.int32, (C, C), 1)
    mask_s[...] = jnp.exp(jnp.abs(n_i - m_i).astype(F32) * lg)
    row = lax.broadcasted_iota(jnp.int32, (C, Dh), 0).astype(F32)
    xif_s[...] = jnp.exp((row + 1.0) * lg)
    xib_s[...] = jnp.exp((C - row) * lg)
    zf_s[...] = jnp.exp((C - 1.0 - row) * lg)
    zb_s[...] = jnp.exp(row * lg)
    decay = jnp.exp(jnp.full((1, Dh), float(C), F32) * lg)

    def chunk_refs(c):
        r = pl.ds(pl.multiple_of(c * C, C), C)
        return r, qr[r, :], kr[r, :], v_ref[r, :]

    def advance(kc, vc, zeta):
        kz = (kc.astype(F32) * zeta).astype(BF16)
        st[...] = st[...] * decay + lax.dot_general(kz, vc, contract_rows, preferred_element_type=F32)

    st[...] = jnp.zeros_like(st)

    def fwd_body(c, carry):
        r, qc, kc, vc = chunk_refs(c)
        s = lax.dot_general(qc, kc, contract_last, preferred_element_type=F32) * mask_s[...]
        y = _bdot(s.astype(BF16), vc)
        y = y + _bdot(qc, st[...].astype(BF16)) * xif_s[...]
        yacc[r, :] = y
        advance(kc, vc, zf_s[...])
        return carry

    lax.fori_loop(0, nc, fwd_body, 0)

    st[...] = jnp.zeros_like(st)

    def bwd_body(i, carry):
        r, qc, kc, vc = chunk_refs(nc - 1 - i)
        y = yacc[r, :] + _bdot(qc, st[...].astype(BF16)) * xib_s[...]
        mu = jnp.mean(y, axis=-1, keepdims=True)
        d = y - mu
        var = jnp.mean(d * d, axis=-1, keepdims=True)
        yn = d * lax.rsqrt(var + EPS) * gn_ref[...]
        gg = g_ref[r, :].astype(F32)
        o_ref[r, :] = (gg * jax.nn.sigmoid(gg) * yn).astype(o_ref.dtype)
        advance(kc, vc, zb_s[...])
        return carry

    lax.fori_loop(0, nc, bwd_body, 0)


def _retention(u_ret, gn_g, batch, seq, n_heads, chunk=RET_CHUNK):
    T, W4 = u_ret.shape
    RW = W4 // 4
    Dh = RW // n_heads
    half = Dh // 2
    H = n_heads
    chunk = min(chunk, seq)
    log_g = jnp.log1p(-(2.0 ** (-5.0 - jnp.arange(H, dtype=F32))))
    inv = ROPE_BASE ** (-jnp.arange(half, dtype=F32) / half)
    ang = jnp.arange(seq, dtype=F32)[:, None] * inv[None, :]
    cos, sin = jnp.cos(ang), jnp.sin(ang)

    def col(off):
        return pl.BlockSpec((seq, Dh), lambda b, h: (b, off + h))

    table = pl.BlockSpec((seq, half), lambda b, h: (0, 0))
    return pl.pallas_call(
        functools.partial(_retention_kernel, chunk=chunk, k_scale=Dh ** -0.5),
        grid=(batch, H),
        in_specs=[pl.BlockSpec(memory_space=pltpu.SMEM), col(0), col(H), col(2 * H), col(3 * H), table, table,
                  pl.BlockSpec((1, Dh), lambda b, h: (0, h))],
        out_specs=pl.BlockSpec((seq, Dh), lambda b, h: (b, h)),
        out_shape=jax.ShapeDtypeStruct((T, RW), BF16),
        scratch_shapes=[
            pltpu.VMEM((seq, Dh), BF16), pltpu.VMEM((seq, Dh), BF16), pltpu.VMEM((seq, Dh), F32),
            pltpu.VMEM((Dh, Dh), F32), pltpu.VMEM((chunk, chunk), F32),
            pltpu.VMEM((chunk, Dh), F32), pltpu.VMEM((chunk, Dh), F32),
            pltpu.VMEM((chunk, Dh), F32), pltpu.VMEM((chunk, Dh), F32),
        ],
        compiler_params=_params(2),
        name="retention",
    )(log_g, u_ret, u_ret, u_ret, u_ret, cos, sin, gn_g.reshape(1, RW))


def _fill_padded(src_ref, xp, rows):
    L, n = src_ref.shape
    zeros = jnp.zeros((HALO, n), xp.dtype)
    xp[0:HALO, :] = zeros
    xp[HALO + L:HALO + L + HALO, :] = zeros

    def body(i, carry):
        r0 = pl.multiple_of(i * rows, rows)
        xp[pl.ds(r0 + HALO, rows), :] = src_ref[pl.ds(r0, rows), :].astype(xp.dtype)
        return carry

    lax.fori_loop(0, L // rows, body, 0)


def _taps(xe, w_ref, b_ref, rows, first):
    acc = b_ref[...]
    for k in range(w_ref.shape[0]):
        o = HALO + first + k
        acc = acc + w_ref[k:k + 1, :] * xe[o:o + rows, :]
    return acc


def _lru_kernel(xr_ref, xg_ref, cw_ref, cb_ref, w_ref, b_ref, lam_ref, o_ref, xp, stage, at, bt, *, rows):
    L, n = xr_ref.shape
    R = rows
    NS = SUBLANES
    S = L // NS
    nj = S // R
    _fill_padded(xr_ref, xp, R)
    nl = -lam_ref[...]
    c8 = -LRU_C * (jnp.maximum(nl, 0.0) + jnp.log1p(jnp.exp(-jnp.abs(nl))))

    def gate_body(jc, carry):
        j0 = pl.multiple_of(jc * R, R)

        def seg_body(s, c):
            r0 = pl.multiple_of(s * S + j0, R)
            xe = xp[pl.ds(r0, R + 2 * HALO), :]
            xc = _taps(xe, cw_ref, cb_ref, R, -2)
            pre = _bdot(xc.astype(BF16), w_ref[...]) + b_ref[...]
            sig = 0.5 * jnp.tanh(0.5 * pre) + 0.5
            for d in range(2):
                r = sig[:, 2 * n * d:2 * n * d + n]
                ig = sig[:, 2 * n * d + n:2 * n * (d + 1)]
                log_a = c8[d:d + 1, :] * r
                a = jnp.exp(log_a)
                stage[2 * d, s] = a
                stage[2 * d + 1, s] = jnp.sqrt(-jnp.tanh(log_a) * (a * a + 1.0)) * (ig * xc)
            return c

        lax.fori_loop(0, NS, seg_body, 0)
        for d in range(2):
            at[d, pl.ds(j0, R)] = pltpu.einshape("sjc->jsc", stage[2 * d])
            bt[d, pl.ds(j0, R)] = pltpu.einshape("sjc->jsc", stage[2 * d + 1])
        return carry

    lax.fori_loop(0, nj, gate_body, 0)

    def totals_body(j, carry):
        h0, p0, h1, p1 = carry
        jr = S - 1 - j
        a0, a1 = at[0, j], at[1, jr]
        return a0 * h0 + bt[0, j], a0 * p0, a1 * h1 + bt[1, jr], a1 * p1

    zero = jnp.zeros((NS, n), F32)
    one = jnp.ones((NS, n), F32)
    h0, p0, h1, p1 = lax.fori_loop(0, S, totals_body, (zero, one, zero, one), unroll=8)

    row = jnp.zeros((1, n), F32)
    rows0 = [row]
    for s in range(1, NS):
        row = h0[s - 1:s] + p0[s - 1:s] * row
        rows0.append(row)
    row = jnp.zeros((1, n), F32)
    rows1 = [row]
    for s in range(NS - 2, -1, -1):
        row = h1[s + 1:s + 2] + p1[s + 1:s + 2] * row
        rows1.insert(0, row)

    def final_body(j, carry):
        h0, h1 = carry
        jr = S - 1 - j
        h0 = at[0, j] * h0 + bt[0, j]
        h1 = at[1, jr] * h1 + bt[1, jr]
        bt[0, j] = h0
        bt[1, jr] = h1
        return h0, h1

    lax.fori_loop(0, S, final_body, (jnp.concatenate(rows0, axis=0), jnp.concatenate(rows1, axis=0)), unroll=8)

    def out_body(jc, carry):
        j0 = pl.multiple_of(jc * R, R)
        hs = pltpu.einshape("jsc->sjc", bt[0, pl.ds(j0, R)] + bt[1, pl.ds(j0, R)])
        for s in range(NS):
            r = pl.ds(pl.multiple_of(s * S + j0, R), R)
            xg = xg_ref[r, :]
            gelu = 0.5 * xg * (1.0 + jnp.tanh(math.sqrt(2.0 / math.pi) * (xg + 0.044715 * (xg * xg * xg))))
            o_ref[r, :] = (hs[s] * gelu).astype(o_ref.dtype)
        return carry

    lax.fori_loop(0, nj, out_body, 0)


def _lru(u_lru, conv_w, conv_b, a_w, a_b, i_w, i_b, lam, batch, seq, rows=LRU_ROWS):
    T, W2 = u_lru.shape
    W = W2 // 2
    nb, n = a_w.shape[1], a_w.shape[2]
    rows = min(rows, seq // SUBLANES)
    w_cat = jnp.concatenate([a_w[0], i_w[0], a_w[1], i_w[1]], axis=-1).astype(BF16)
    b_cat = jnp.stack([a_b[0], i_b[0], a_b[1], i_b[1]], axis=0).reshape(4, nb, n).transpose(1, 0, 2).reshape(nb, 1, 4 * n)
    return pl.pallas_call(
        functools.partial(_lru_kernel, rows=rows),
        grid=(batch, nb),
        in_specs=[
            pl.BlockSpec((seq, n), lambda b, j: (b, j)),
            pl.BlockSpec((seq, n), lambda b, j: (b, nb + j)),
            pl.BlockSpec((conv_w.shape[0], n), lambda b, j: (0, j)),
            pl.BlockSpec((1, n), lambda b, j: (0, j)),
            pl.BlockSpec((None, n, 4 * n), lambda b, j: (j, 0, 0)),
            pl.BlockSpec((None, 1, 4 * n), lambda b, j: (j, 0, 0)),
            pl.BlockSpec((2, n), lambda b, j: (0, j)),
        ],
        out_specs=pl.BlockSpec((seq, n), lambda b, j: (b, j)),
        out_shape=jax.ShapeDtypeStruct((T, W), BF16),
        scratch_shapes=[pltpu.VMEM((seq + 2 * HALO, n), F32), pltpu.VMEM((4, SUBLANES, rows, n), F32),
                        pltpu.VMEM((2, seq // SUBLANES, SUBLANES, n), F32),
                        pltpu.VMEM((2, seq // SUBLANES, SUBLANES, n), F32)],
        compiler_params=_params(2),
        name="rglru",
    )(u_lru, u_lru, conv_w, conv_b.reshape(1, W), w_cat, b_cat, lam)


def _conv3_kernel(u_ref, w_ref, b_ref, o_ref, xp, *, rows):
    L, n = u_ref.shape
    _fill_padded(u_ref, xp, rows)

    def body(i, carry):
        r0 = pl.multiple_of(i * rows, rows)
        xe = xp[pl.ds(r0, rows + 2 * HALO), :]
        o_ref[pl.ds(r0, rows), :] = _taps(xe, w_ref, b_ref, rows, -1).astype(o_ref.dtype)
        return carry

    lax.fori_loop(0, L // rows, body, 0)


def _conv3(u, w, b, batch, seq, parts=3, tn=512, rows=256):
    T, N = u.shape
    W = N // parts
    rows = min(rows, seq)
    nw = W // tn
    return pl.pallas_call(
        functools.partial(_conv3_kernel, rows=rows),
        grid=(batch, parts, nw),
        in_specs=[pl.BlockSpec((seq, tn), lambda bi, p, j: (bi, p * nw + j)),
                  pl.BlockSpec((w.shape[0], tn), lambda bi, p, j: (0, p * nw + j)),
                  pl.BlockSpec((1, tn), lambda bi, p, j: (0, p * nw + j))],
        out_specs=pl.BlockSpec((None, seq, tn), lambda bi, p, j: (p, bi, j)),
        out_shape=jax.ShapeDtypeStruct((parts, T, W), BF16),
        scratch_shapes=[pltpu.VMEM((seq + 2 * HALO, tn), F32)],
        compiler_params=_params(3),
        name="conv3",
    )(u, w, b.reshape(1, N))


def _filter_kernel(z_ref, w1_ref, b1_ref, w2_ref, b2_ref, fr_ref, wf_ref, wb_ref, dl_ref, e_ref, o_ref, hid, *, rows):
    L = z_ref.shape[0]
    hi = lax.Precision.HIGHEST

    @pl.when((pl.program_id(0) == 0) & (pl.program_id(1) == 0))
    def _():
        def body(i, carry):
            r = pl.ds(pl.multiple_of(i * rows, rows), rows)
            h1 = jnp.sin(fr_ref[0:1, :] * (jnp.dot(z_ref[r, :], w1_ref[...], precision=hi, preferred_element_type=F32) + b1_ref[...]))
            h2 = jnp.sin(fr_ref[1:2, :] * (jnp.dot(h1, w2_ref[...], precision=hi, preferred_element_type=F32) + b2_ref[...]))
            hid[r, :] = h2
            return carry

        lax.fori_loop(0, L // rows, body, 0)

    def body(i, carry):
        r0 = pl.multiple_of(i * rows, rows)
        r = pl.ds(r0, rows)
        hh = hid[r, :].astype(BF16)
        t = z_ref[r, 0:1]
        win = jnp.exp(-t * jnp.abs(dl_ref[...]))
        hf = _bdot(hh, wf_ref[...].astype(BF16)) * win
        hb = _bdot(hh, wb_ref[...].astype(BF16)) * win
        lag = lax.broadcasted_iota(jnp.int32, hb.shape, 0) + r0
        hb = jnp.where(lag == 0, 0.0, hb)
        e_ref[r, :] = (hf + hb).astype(e_ref.dtype)
        o_ref[r, :] = (hf - hb).astype(o_ref.dtype)
        return carry

    lax.fori_loop(0, L // rows, body, 0)


def _hyena_filters(seq, width, w1, b1, w2, b2, w3, freq, tn=512, rows=512):
    emb, hid_in = w1.shape
    order = w3.shape[1] // (2 * width)
    rows = min(rows, seq)
    tn = min(tn, width)
    nt = width // tn
    t = jnp.linspace(0.0, 1.0, seq, dtype=F32)[:, None]
    bands = (emb - 1) // 2
    f = jnp.linspace(1e-4, bands - 1, bands, dtype=F32)[None, :]
    w = 2.0 * math.pi * jnp.arange(seq, dtype=F32)[:, None] / seq
    z = jnp.concatenate([t, jnp.cos(f * w), -jnp.sin(f * w)], axis=-1)
    hidden = LANES
    pad = lambda a, shape: jnp.zeros(shape, F32).at[tuple(slice(0, d) for d in a.shape)].set(a)
    zp = pad(z, (seq, LANES))
    w1p = pad(w1, (LANES, hidden))
    b1, b2 = pad(b1[None], (1, hidden)), pad(b2[None], (1, hidden))
    w2 = pad(w2, (hidden, hidden))
    w3 = pad(w3, (hidden, w3.shape[1]))
    freq = pad(freq, (2, hidden))
    deltas = jnp.linspace(math.log(HY_TARGET) / HY_SLOW_DECAY, math.log(HY_TARGET) / HY_FAST_DECAY, width, dtype=F32)
    full = lambda shape: pl.BlockSpec(shape, lambda o, j: (0,) * len(shape))
    out = pl.BlockSpec((None, seq, tn), lambda o, j: (o, 0, j))
    return pl.pallas_call(
        functools.partial(_filter_kernel, rows=rows),
        grid=(order, nt),
        in_specs=[
            full((seq, LANES)), full((LANES, hidden)), full((1, hidden)), full((hidden, hidden)), full((1, hidden)),
            full((2, hidden)),
            pl.BlockSpec((hidden, tn), lambda o, j: (0, o * 2 * nt + j)),
            pl.BlockSpec((hidden, tn), lambda o, j: (0, o * 2 * nt + nt + j)),
            pl.BlockSpec((1, tn), lambda o, j: (0, j)),
        ],
        out_specs=[out, out],
        out_shape=[jax.ShapeDtypeStruct((order, seq, width), BF16)] * 2,
        scratch_shapes=[pltpu.VMEM((seq, hidden), F32)],
        compiler_params=_params(2),
        name="hyena_filter",
    )(zp, w1p, b1.reshape(1, hidden), w2, b2.reshape(1, hidden), freq, w3, w3, deltas.reshape(1, width))


def _fft_tables(seq, r2=FFT_R2):
    r2 = min(r2, seq // 16)
    r1 = seq // r2
    a1 = jnp.arange(r1, dtype=jnp.int32)
    a2 = jnp.arange(r2, dtype=jnp.int32)
    ang1 = (((2 * a1[:, None] + 1) * a1[None, :]) % (4 * r1)).astype(F32) * (2.0 * math.pi / (4 * r1))
    c1, s1 = jnp.cos(ang1), jnp.sin(ang1)
    fwd1 = jnp.concatenate([c1, -s1], axis=0).astype(BF16)
    inv1 = jnp.concatenate([c1.T, -s1.T], axis=1).astype(BF16)
    k = a1[:, None, None] + 2 * r1 * a2[None, :, None]
    ang2 = (((2 * k + 1) * a2[None, None, :]) % (4 * seq)).astype(F32) * (2.0 * math.pi / (4 * seq))
    c2, s2 = jnp.cos(ang2), jnp.sin(ang2)
    fwd2 = jnp.concatenate([jnp.concatenate([c2, s2], axis=2), jnp.concatenate([-s2, c2], axis=2)], axis=1)
    fwd2 = fwd2.astype(BF16)
    return dict(r1=r1, r2=r2, fwd1=fwd1, inv1=inv1, fwd2=fwd2, inv2=jnp.swapaxes(fwd2, 1, 2))


def _time_spec(lead, r1, n, tn):
    lead = tuple(lead)
    return pl.BlockSpec((None,) * (len(lead) + 1) + (r1, n, tn), lambda b, s, j: lead + (b, 0, s, j))


def _fft_s1_kernel(m_ref, x_ref, o_ref):
    n = x_ref.shape[1]
    xt = pltpu.einshape("tjc->jtc", x_ref[...])
    y = jnp.stack([_bdot(m_ref[...], xt[j]) for j in range(n)], axis=0).astype(o_ref.dtype)
    o_ref[...] = pltpu.einshape("jkc->kjc", y)


def _fft_stage1(tab, x, lead, width, n=16, tn=2048):
    r1, r2 = tab["r1"], tab["r2"]
    B = x.shape[len(lead)]
    n, tn = min(n, r2), min(tn, width)
    return pl.pallas_call(
        _fft_s1_kernel,
        grid=(B, r2 // n, width // tn),
        in_specs=[pl.BlockSpec((2 * r1, r1), lambda b, s, j: (0, 0)), _time_spec(lead, r1, n, tn)],
        out_specs=_time_spec((), 2 * r1, n, tn),
        out_shape=jax.ShapeDtypeStruct((B, 2 * r1, r2, width), BF16),
        compiler_params=_params(3),
        name="hyena_fft_stage1",
    )(tab["fwd1"], x)


def _k1_specs(r2, n, tn):
    return [pl.BlockSpec((None, 2, None, r2, tn), lambda b, s, j, i=i: (b, 0, s * n + i, 0, j)) for i in range(n)]


def _fft_spec_kernel(a_ref, *refs, n):
    ye, yo, o_ref = refs[:n], refs[n:2 * n], refs[2 * n]
    r2 = a_ref.shape[1] // 2
    for i in range(n):
        xe = _bdot(a_ref[i], ye[i][...].reshape(2 * r2, -1))
        xo = _bdot(a_ref[i], yo[i][...].reshape(2 * r2, -1))
        o_ref[i, :r2, :] = xe[:r2].astype(o_ref.dtype)
        o_ref[i, r2:, :] = xo[r2:].astype(o_ref.dtype)


def _fft_filter_spectrum(tab, ye, yo, n=8, tn=1024):
    r1, r2 = tab["r1"], tab["r2"]
    orders, W = ye.shape[0], ye.shape[-1]
    n, tn = min(n, r1), min(tn, W)
    return pl.pallas_call(
        functools.partial(_fft_spec_kernel, n=n),
        grid=(orders, r1 // n, W // tn),
        in_specs=[pl.BlockSpec((n, 2 * r2, 2 * r2), lambda b, s, j: (s, 0, 0))] + _k1_specs(r2, n, tn) * 2,
        out_specs=pl.BlockSpec((None, n, 2 * r2, tn), lambda b, s, j: (b, s, 0, j)),
        out_shape=jax.ShapeDtypeStruct((orders, r1, 2 * r2, W), BF16),
        compiler_params=_params(3),
        name="hyena_fft_spectrum",
    )(tab["fwd2"], *([ye] * n), *([yo] * n))


def _fft_mid_kernel(a_ref, at_ref, k_ref, *refs, n, scale):
    ys, o_ref = refs[:n], refs[n]
    r2 = a_ref.shape[1] // 2
    for i in range(n):
        x = _bdot(a_ref[i], ys[i][...].reshape(2 * r2, -1))
        xr, xi = x[:r2], x[r2:]
        kr = k_ref[i, :r2, :].astype(F32) * scale
        ki = k_ref[i, r2:, :].astype(F32) * scale
        p = jnp.concatenate([xr * kr - xi * ki, xr * ki + xi * kr], axis=0).astype(BF16)
        q = _bdot(at_ref[i], p).astype(o_ref.dtype)
        o_ref[0, i] = q[:r2]
        o_ref[1, i] = q[r2:]


def _fft_mid(tab, y, kspec, order, n=8, tn=1024):
    r1, r2 = tab["r1"], tab["r2"]
    B, W = y.shape[0], y.shape[-1]
    n, tn = min(n, r1), min(tn, W)
    mat = pl.BlockSpec((n, 2 * r2, 2 * r2), lambda s, j, b: (s, 0, 0))
    y_specs = [pl.BlockSpec((None, 2, None, r2, tn), lambda s, j, b, i=i: (b, 0, s * n + i, 0, j)) for i in range(n)]
    return pl.pallas_call(
        functools.partial(_fft_mid_kernel, n=n, scale=1.0 / (r1 * r2)),
        grid=(r1 // n, W // tn, B),
        in_specs=[mat, mat, pl.BlockSpec((None, n, 2 * r2, tn), lambda s, j, b: (order, s, 0, j))] + y_specs,
        out_specs=pl.BlockSpec((None, 2, n, r2, tn), lambda s, j, b: (b, 0, s, 0, j)),
        out_shape=jax.ShapeDtypeStruct((B, 2, r1, r2, W), BF16),
        compiler_params=_params(3),
        name="hyena_fft_mid",
    )(tab["fwd2"], tab["inv2"], kspec, *([y] * n))


def _fft_out_kernel(m_ref, bias_ref, q_ref, x_ref, g_ref, o_ref):
    n = q_ref.shape[1]
    qt = pltpu.einshape("kjc->jkc", q_ref[...])
    y = jnp.stack([_bdot(m_ref[...], qt[j]) for j in range(n)], axis=0)
    y = pltpu.einshape("jtc->tjc", y)
    x = x_ref[...].astype(F32)
    o_ref[...] = (g_ref[...].astype(F32) * (y + bias_ref[...] * x)).astype(o_ref.dtype)


def _fft_out(tab, q, x, x_lead, gate, gate_lead, bias, n=16, tn=2048):
    r1, r2 = tab["r1"], tab["r2"]
    B = q.shape[0]
    W = bias.shape[0]
    n, tn = min(n, r2), min(tn, W)
    return pl.pallas_call(
        _fft_out_kernel,
        grid=(B, r2 // n, W // tn),
        in_specs=[pl.BlockSpec((r1, 2 * r1), lambda b, s, j: (0, 0)), pl.BlockSpec((1, tn), lambda b, s, j: (0, j)),
                  _time_spec((), 2 * r1, n, tn), _time_spec(x_lead, r1, n, tn), _time_spec(gate_lead, r1, n, tn)],
        out_specs=_time_spec((), r1, n, tn),
        out_shape=jax.ShapeDtypeStruct((B, r1, r2, W), BF16),
        compiler_params=_params(3),
        name="hyena_fft_out",
    )(tab["inv1"], bias.reshape(1, W), q, x, gate)


def _hyena_mix(h, tab, w_in, w_idx, b_in, conv_w, conv_b, f_w1, f_b1, f_w2, f_b2, f_w3, f_freq, bias, batch, seq):
    T = h.shape[0]
    W = w_in.shape[-1] // 3
    r1, r2 = tab["r1"], tab["r2"]
    u = _matmul(h, w_in, w_idx, b_in, out_dtype=BF16)
    uc = _conv3(u, conv_w, conv_b, batch, seq).reshape(3, batch, r1, r2, W)
    e, o = _hyena_filters(seq, W, f_w1, f_b1, f_w2, f_b2, f_w3, f_freq)
    orders = e.shape[0]
    split = lambda y: y.reshape(y.shape[0], 2, r1, r2, W)
    ye = _fft_stage1(tab, e.reshape(orders, r1, r2, W), (), W)
    yo = _fft_stage1(tab, o.reshape(orders, r1, r2, W), (), W)
    kspec = _fft_filter_spectrum(tab, split(ye), split(yo))
    z, z_lead = uc, (0,)
    for order in range(orders):
        y = _fft_stage1(tab, z, z_lead, W)
        q = _fft_mid(tab, split(y), kspec, order).reshape(batch, 2 * r1, r2, W)
        z, z_lead = _fft_out(tab, q, z, z_lead, uc, (order + 1,), bias[order]), ()
    return z.reshape(T, W)


def kernel(x, c, cond_w, cond_b, mod_w, mod_b, norm_g, ffn_w1, ffn_w3, ffn_w2, ev_in_w, ev_out_w, ret_gn_g, lru_conv_w, lru_conv_b, lru_a_w, lru_a_b, lru_i_w, lru_i_b, lru_lambda, hy_in_w, hy_in_b, hy_conv_w, hy_conv_b, hy_f_w1, hy_f_b1, hy_f_w2, hy_f_b2, hy_f_w3, hy_f_freq, hy_bias, hy_out_w, final_g):
    B, L, D = x.shape
    depth = mod_w.shape[0]
    T = B * L
    mods = _conditioning(c, cond_w, cond_b, mod_w, mod_b)
    tab = _fft_tables(L) if depth > 1 else None
    xt = x.reshape(T, D)

    def ffn(xt, layer, which, mod, idx):
        shift, scale, gate = (mod[:, j:j + 1, :] for j in idx)
        h = _norm_mod(xt, norm_g[layer, 2 * which], scale, shift, L)
        u = _ffn_up(h, ffn_w1, ffn_w3, (layer, which))
        return _res_mm([u], ffn_w2, (layer, which), xt, gate, L, 0.5)

    for layer in range(depth):
        mod = mods[layer]
        xt = ffn(xt, layer, 0, mod, (0, 1, 2))
        h = _norm_mod(xt, norm_g[layer, 1], mod[:, 4:5, :], mod[:, 3:4, :], L)
        gate = mod[:, 5:6, :]
        if layer % 2 == 0:
            e = layer // 2
            RW = ret_gn_g.shape[1]
            u_ret = _matmul(h, ev_in_w, (e,), w_col=0, n_out=4 * RW, out_dtype=BF16)
            u_lru = _matmul(h, ev_in_w, (e,), w_col=4 * RW, out_dtype=F32)
            ret = _retention(u_ret, ret_gn_g[e], B, L, RET_HEADS)
            lru = _lru(u_lru, lru_conv_w[e], lru_conv_b[e], lru_a_w[e], lru_a_b[e], lru_i_w[e], lru_i_b[e],
                       lru_lambda[e], B, L)
            xt = _res_mm([ret, lru], ev_out_w, (e,), xt, gate, L, 1.0, tn=512)
        else:
            o = layer // 2
            z = _hyena_mix(h, tab, hy_in_w, (o,), hy_in_b[o], hy_conv_w[o], hy_conv_b[o], hy_f_w1[o], hy_f_b1[o],
                           hy_f_w2[o], hy_f_b2[o], hy_f_w3[o], hy_f_freq[o], hy_bias[o], B, L)
            xt = _res_mm([z], hy_out_w, (o,), xt, gate, L, 1.0, tn=512)
        xt = ffn(xt, layer, 1, mod, (6, 7, 8))
    return _final_norm(xt, final_g).reshape(B, L, D)
```
